```python
import math
import jax, jax.numpy as jnp
from jax import lax
import numpy as np

D_MODEL = 2048
BATCH = 2
SEQ = 8192
DEPTH = 1

N_Q_HEADS = 16
N_KV_HEADS = 2
HEAD_DIM = 64
Q_PER_KV = N_Q_HEADS // N_KV_HEADS
WINDOW = 128
BLOCK = 128
ATTN_WIDTH = N_Q_HEADS * HEAD_DIM
KV_WIDTH = N_KV_HEADS * HEAD_DIM
SSM_GROUP = 16
SSM_GROUPS = 32
SSM_WIDTH = SSM_GROUP * SSM_GROUPS
SSM_STATE = 64
DT_MIN = 0.001
DT_MAX = 0.1
D_FF = 5632
CONV_WIDTH = 3
RMS_EPS = 1e-6
IN_COLS = ATTN_WIDTH + 2 * KV_WIDTH + SSM_WIDTH + 2 * D_MODEL
SPLIT_POINTS = (ATTN_WIDTH, ATTN_WIDTH + KV_WIDTH, ATTN_WIDTH + 2 * KV_WIDTH,
                ATTN_WIDTH + 2 * KV_WIDTH + SSM_WIDTH,
                ATTN_WIDTH + 2 * KV_WIDTH + SSM_WIDTH + D_MODEL)
NEG_BIG = -1e30

kernel_name = "hybrid_swa_s5_convffn_block"


def rms_norm(x, g):
    xf = x.astype(jnp.float32)
    y = xf * lax.rsqrt(jnp.mean(xf * xf, axis=-1, keepdims=True) + RMS_EPS)
    return (y * g.astype(jnp.float32)).astype(x.dtype)


def sliding_window_attention(q, k, v, sinks):
    b, l = q.shape[0], q.shape[1]
    nb = l // BLOCK
    qb = q.reshape(b, nb, BLOCK, N_KV_HEADS, Q_PER_KV, HEAD_DIM)
    kb = k.reshape(b, nb, BLOCK, N_KV_HEADS, HEAD_DIM)
    vb = v.reshape(b, nb, BLOCK, N_KV_HEADS, HEAD_DIM)

    def with_prev(t):
        prev = jnp.pad(t, ((0, 0), (1, 0), (0, 0), (0, 0), (0, 0)))[:, :-1]
        return jnp.concatenate([prev, t], axis=2)

    kx = with_prev(kb)
    vx = with_prev(vb)
    scores = jnp.einsum('bnqgrd,bnsgd->bngrqs', qb, kx).astype(jnp.float32) * (HEAD_DIM ** -0.5)
    q_idx = jnp.arange(BLOCK)[:, None]
    s_idx = jnp.arange(2 * BLOCK)[None, :]
    dist = q_idx + BLOCK - s_idx
    band = (dist >= 0) & (dist < WINDOW)
    valid = band[None] & ((jnp.arange(nb)[:, None, None] > 0) | (s_idx[None] >= BLOCK))
    slopes = 2.0 ** (-8.0 * jnp.arange(1, N_Q_HEADS + 1, dtype=jnp.float32) / N_Q_HEADS)
    slopes = slopes.reshape(N_KV_HEADS, Q_PER_KV)
    alibi = -slopes[:, :, None, None] * dist.astype(jnp.float32)[None, None]
    scores = scores + alibi[None, None]
    scores = jnp.where(valid[None, :, None, None], scores, NEG_BIG)
    sink = sinks.astype(jnp.float32).reshape(N_KV_HEADS, Q_PER_KV)[None, None, :, :, None, None]
    m = jnp.maximum(jnp.max(scores, axis=-1, keepdims=True), sink)
    p = jnp.exp(scores - m)
    p = p / (jnp.sum(p, axis=-1, keepdims=True) + jnp.exp(sink - m))
    out = jnp.einsum('bngrqs,bnsgd->bnqgrd', p.astype(v.dtype), vx)
    return out.reshape(b, l, ATTN_WIDTH)


def s5_ssm(u, a_re, a_im, log_dt, b_re, b_im, c_re, c_im, d_skip):
    bsz, l = u.shape[0], u.shape[1]
    ug = u.reshape(bsz, l, SSM_GROUPS, SSM_GROUP)
    dt = jnp.exp(log_dt)[:, None]
    mag = jnp.exp(a_re * dt)
    ab_re = mag * jnp.cos(a_im * dt)
    ab_im = mag * jnp.sin(a_im * dt)
    nr = ab_re - 1.0
    ni = ab_im
    den = a_re * a_re + a_im * a_im
    z_re = (nr * a_re + ni * a_im) / den
    z_im = (ni * a_re - nr * a_im) / den
    bb_re = z_re[..., None] * b_re - z_im[..., None] * b_im
    bb_im = z_re[..., None] * b_im + z_im[..., None] * b_re
    bu_re = jnp.einsum('gph,blgh->blgp', bb_re, ug)
    bu_im = jnp.einsum('gph,blgh->blgp', bb_im, ug)
    a_re_t = jnp.broadcast_to(ab_re, bu_re.shape)
    a_im_t = jnp.broadcast_to(ab_im, bu_im.shape)

    def combine(left, right):
        a1r, a1i, b1r, b1i = left
        a2r, a2i, b2r, b2i = right
        return (a1r * a2r - a1i * a2i,
                a1r * a2i + a1i * a2r,
                a2r * b1r - a2i * b1i + b2r,
                a2r * b1i + a2i * b1r + b2i)

    _, _, xs_re, xs_im = lax.associative_scan(combine, (a_re_t, a_im_t, bu_re, bu_im), axis=1)
    y = (jnp.einsum('ghp,blgp->blgh', c_re, xs_re)
         - jnp.einsum('ghp,blgp->blgh', c_im, xs_im)
         + d_skip.reshape(SSM_GROUPS, SSM_GROUP) * ug)
    return y.reshape(bsz, l, SSM_WIDTH)


def causal_depthwise_conv(x, w, b):
    l = x.shape[1]
    xp = jnp.pad(x, ((0, 0), (CONV_WIDTH - 1, 0), (0, 0)))
    y = b
    for k in range(CONV_WIDTH):
        y = y + w[k] * xp[:, k:k + l]
    return y


def setup_inputs(seed: int = 0) -> dict:
    key = jax.random.key(seed)
    ks = jax.random.split(key, 24)
    f32 = jnp.float32
    nrm = lambda k, shape, s: jax.random.normal(k, shape, f32) * s
    x = jax.random.normal(ks[0], (BATCH, SEQ, D_MODEL), f32)
    attn_norm_g = 1.0 + nrm(ks[1], (DEPTH, D_MODEL), 0.02)
    w_in = nrm(ks[2], (DEPTH, D_MODEL, IN_COLS), D_MODEL ** -0.5)
    b_in = nrm(ks[3], (DEPTH, IN_COLS), 0.02)
    attn_sinks = nrm(ks[4], (DEPTH, N_Q_HEADS), 0.5)
    ssm_a_re = -0.5 + nrm(ks[5], (DEPTH, SSM_GROUPS, SSM_STATE), 0.01)
    ssm_a_im = (jnp.pi * jnp.arange(SSM_STATE, dtype=f32))[None, None, :] + nrm(ks[6], (DEPTH, SSM_GROUPS, SSM_STATE), 0.01)
    ssm_log_dt = jax.random.uniform(ks[7], (DEPTH, SSM_GROUPS), f32, minval=math.log(DT_MIN), maxval=math.log(DT_MAX))
    ssm_b_re = nrm(ks[8], (DEPTH, SSM_GROUPS, SSM_STATE, SSM_GROUP), (2 * SSM_GROUP) ** -0.5)
    ssm_b_im = nrm(ks[9], (DEPTH, SSM_GROUPS, SSM_STATE, SSM_GROUP), (2 * SSM_GROUP) ** -0.5)
    ssm_c_re = nrm(ks[10], (DEPTH, SSM_GROUPS, SSM_GROUP, SSM_STATE), (2 * SSM_STATE) ** -0.5)
    ssm_c_im = nrm(ks[11], (DEPTH, SSM_GROUPS, SSM_GROUP, SSM_STATE), (2 * SSM_STATE) ** -0.5)
    ssm_d = nrm(ks[12], (DEPTH, SSM_WIDTH), 1.0)
    w_glu = nrm(ks[13], (DEPTH, SSM_WIDTH, 2 * SSM_WIDTH), SSM_WIDTH ** -0.5)
    b_glu = nrm(ks[14], (DEPTH, 2 * SSM_WIDTH), 0.02)
    w_branch_attn = nrm(ks[15], (DEPTH, ATTN_WIDTH, D_MODEL), ATTN_WIDTH ** -0.5)
    w_branch_ssm = nrm(ks[16], (DEPTH, SSM_WIDTH, D_MODEL), SSM_WIDTH ** -0.5)
    w_out = nrm(ks[17], (DEPTH, D_MODEL, D_MODEL), D_MODEL ** -0.5)
    ffn_norm_g = 1.0 + nrm(ks[18], (DEPTH, D_MODEL), 0.02)
    w_up = nrm(ks[19], (DEPTH, D_MODEL, 2 * D_FF), D_MODEL ** -0.5)
    conv_w = nrm(ks[20], (DEPTH, CONV_WIDTH, D_FF), CONV_WIDTH ** -0.5)
    conv_b = nrm(ks[21], (DEPTH, D_FF), 0.02)
    w_down = nrm(ks[22], (DEPTH, D_FF, D_MODEL), D_FF ** -0.5)
    final_norm_g = 1.0 + nrm(ks[23], (D_MODEL,), 0.02)
    return {"x": x, "attn_norm_g": attn_norm_g, "w_in": w_in, "b_in": b_in, "attn_sinks": attn_sinks,
            "ssm_a_re": ssm_a_re, "ssm_a_im": ssm_a_im, "ssm_log_dt": ssm_log_dt,
            "ssm_b_re": ssm_b_re, "ssm_b_im": ssm_b_im, "ssm_c_re": ssm_c_re, "ssm_c_im": ssm_c_im,
            "ssm_d": ssm_d, "w_glu": w_glu, "b_glu": b_glu, "w_branch_attn": w_branch_attn,
            "w_branch_ssm": w_branch_ssm, "w_out": w_out, "ffn_norm_g": ffn_norm_g, "w_up": w_up,
            "conv_w": conv_w, "conv_b": conv_b, "w_down": w_down, "final_norm_g": final_norm_g}


def reference(x, attn_norm_g, w_in, b_in, attn_sinks, ssm_a_re, ssm_a_im, ssm_log_dt, ssm_b_re, ssm_b_im,
              ssm_c_re, ssm_c_im, ssm_d, w_glu, b_glu, w_branch_attn, w_branch_ssm, w_out, ffn_norm_g,
              w_up, conv_w, conv_b, w_down, final_norm_g):
    bsz, seq = x.shape[0], x.shape[1]
    for i in range(DEPTH):
        h = rms_norm(x, attn_norm_g[i])
        proj = h @ w_in[i] + b_in[i]
        q, k, v, u, gate_attn, gate_ssm = jnp.split(proj, SPLIT_POINTS, axis=-1)
        q = q.reshape(bsz, seq, N_Q_HEADS, HEAD_DIM)
        k = k.reshape(bsz, seq, N_KV_HEADS, HEAD_DIM)
        v = v.reshape(bsz, seq, N_KV_HEADS, HEAD_DIM)
        attn = sliding_window_attention(q, k, v, attn_sinks[i])
        y = s5_ssm(u, ssm_a_re[i], ssm_a_im[i], ssm_log_dt[i], ssm_b_re[i], ssm_b_im[i],
                   ssm_c_re[i], ssm_c_im[i], ssm_d[i])
        y_val, y_gate = jnp.split(jax.nn.gelu(y, approximate=False) @ w_glu[i] + b_glu[i], 2, axis=-1)
        ssm = y_val * jax.nn.sigmoid(y_gate)
        merged = (jax.nn.sigmoid(gate_attn) * (attn @ w_branch_attn[i])
                  + jax.nn.sigmoid(gate_ssm) * (ssm @ w_branch_ssm[i]))
        x = x + merged @ w_out[i]
        h = rms_norm(x, ffn_norm_g[i])
        val, gate = jnp.split(h @ w_up[i], 2, axis=-1)
        gate = causal_depthwise_conv(gate, conv_w[i], conv_b[i])
        x = x + (val * jax.nn.gelu(gate, approximate=False)) @ w_down[i]
    return rms_norm(x, final_norm_g)
```

```python
import functools
import math

import jax
import jax.numpy as jnp
import numpy as np
from jax import lax
from jax.experimental import pallas as pl
from jax.experimental.pallas import tpu as pltpu

D_MODEL = 2048
N_Q_HEADS = 16
N_KV_HEADS = 2
HEAD_DIM = 64
Q_PER_KV = N_Q_HEADS // N_KV_HEADS
WINDOW = 128
BLOCK = 128
ATTN_WIDTH = N_Q_HEADS * HEAD_DIM
KV_WIDTH = N_KV_HEADS * HEAD_DIM
SSM_GROUP = 16
SSM_GROUPS = 32
SSM_WIDTH = SSM_GROUP * SSM_GROUPS
SSM_STATE = 64
D_FF = 5632
CONV_WIDTH = 3
RMS_EPS = 1e-6
IN_COLS = ATTN_WIDTH + 2 * KV_WIDTH + SSM_WIDTH + 2 * D_MODEL
NEG_BIG = -1e30

V7X_LANES = 128
V7X_SUBLANES = 8
V7X_VMEM_BYTES = 64 * 1024 * 1024

BF16 = jnp.bfloat16
F32 = jnp.float32

_Q0 = 0
_K0 = ATTN_WIDTH
_V0 = _K0 + KV_WIDTH
_U0 = _V0 + KV_WIDTH
_GA0 = _U0 + SSM_WIDTH
_GS0 = _GA0 + D_MODEL

SCAN_CHUNKS = V7X_SUBLANES
SCAN_LEN = 64
SCAN_ROWS = SCAN_CHUNKS * SCAN_LEN
SSM_LANE_GROUPS = 4
LG_STATES = SSM_GROUPS * SSM_STATE // SSM_LANE_GROUPS
LG_CHANNELS = SSM_WIDTH // SSM_LANE_GROUPS


def _dot(a, b):
    return jnp.dot(a, b, preferred_element_type=F32)


def _dot_nt(a, b):
    return lax.dot_general(a, b, (((1,), (1,)), ((), ())), preferred_element_type=F32)


def _rms_norm(x, g):
    return x * lax.rsqrt(jnp.mean(x * x, axis=-1, keepdims=True) + RMS_EPS) * g


def _gelu(x):
    return 0.5 * x * (1.0 + lax.erf(x * np.float32(math.sqrt(0.5))))


def _const_spec(shape):
    nd = len(shape)
    return pl.BlockSpec(shape, lambda *_: (0,) * nd, pipeline_mode=pl.Buffered(1))


def _in_proj_kernel(x_ref, g_ref, w_ref, b_ref, q_ref, kv_ref, u_ref, ga_ref, gs_ref):
    h = _rms_norm(x_ref[...], g_ref[...]).astype(BF16)
    proj = _dot(h, w_ref[...]) + b_ref[...]
    q_ref[...] = (proj[:, _Q0:_K0] * (HEAD_DIM ** -0.5)).astype(BF16)
    kv_ref[...] = proj[:, _K0:_U0].astype(BF16)
    u_ref[...] = proj[:, _U0:_GA0]
    ga_ref[...] = proj[:, _GA0:_GS0]
    gs_ref[...] = proj[:, _GS0:]


def _in_proj(x2, g, w, b, tm):
    t = x2.shape[0]
    row = lambda width: pl.BlockSpec((tm, width), lambda i: (i, 0))
    return pl.pallas_call(
        _in_proj_kernel,
        grid=(t // tm,),
        in_specs=[row(D_MODEL), _const_spec((1, D_MODEL)),
                  _const_spec((D_MODEL, IN_COLS)), _const_spec((1, IN_COLS))],
        out_specs=[row(ATTN_WIDTH), row(2 * KV_WIDTH), row(SSM_WIDTH),
                   row(D_MODEL), row(D_MODEL)],
        out_shape=[jax.ShapeDtypeStruct((t, ATTN_WIDTH), BF16),
                   jax.ShapeDtypeStruct((t, 2 * KV_WIDTH), BF16),
                   jax.ShapeDtypeStruct((t, SSM_WIDTH), F32),
                   jax.ShapeDtypeStruct((t, D_MODEL), F32),
                   jax.ShapeDtypeStruct((t, D_MODEL), F32)],
        compiler_params=pltpu.CompilerParams(
            dimension_semantics=("arbitrary",), vmem_limit_bytes=56 * 1024 * 1024),
        name="in_proj",
    )(x2, g, w, b)


ATTN_TQ = 512
ATTN_SUB = ATTN_TQ // BLOCK


def _attn_bias_table():
    q_idx = np.arange(BLOCK)[:, None]
    s_idx = np.arange(2 * BLOCK)[None, :]
    dist = q_idx + BLOCK - s_idx
    band = (dist >= 0) & (dist < WINDOW)
    slopes = 2.0 ** (-8.0 * np.arange(1, N_Q_HEADS + 1, dtype=np.float32) / N_Q_HEADS)
    alibi = -slopes[:, None, None].astype(np.float32) * dist.astype(np.float32)[None]
    general = np.where(band[None], alibi, np.float32(NEG_BIG))
    first = np.where((band & (s_idx >= BLOCK))[None], alibi, np.float32(NEG_BIG))
    return jnp.asarray(np.stack([general, first]).astype(np.float32))


def _attn_kernel(sink_ref, q_ref, kvc_ref, kvp_ref, bias_ref, o_ref, kbuf):
    n = pl.program_id(1)
    kbuf[0:BLOCK, :] = kvp_ref[...]
    kbuf[BLOCK:, :] = kvc_ref[...]
    lane = lax.broadcasted_iota(jnp.int32, (1, 2 * HEAD_DIM), 1)
    lo = lane < HEAD_DIM

    def sub_block(sb, carry):
        r0 = pl.multiple_of(sb * BLOCK, BLOCK)
        first = jnp.where(jnp.logical_and(n == 0, sb == 0), 1, 0)
        keys = kbuf[pl.ds(r0, 2 * BLOCK), :]
        k_nat = keys[:, :KV_WIDTH].astype(F32)
        v_nat = keys[:, KV_WIDTH:].astype(F32)
        k_swp = pltpu.roll(k_nat, HEAD_DIM, 1)
        v_swp = pltpu.roll(v_nat, HEAD_DIM, 1)
        k_nat = k_nat.astype(BF16)
        k_swp = k_swp.astype(BF16)
        v_even = (jnp.where(lo, v_nat, 1.0).astype(BF16), jnp.where(lo, v_swp, 1.0).astype(BF16))
        v_odd = (jnp.where(lo, 1.0, v_swp).astype(BF16), jnp.where(lo, 1.0, v_nat).astype(BF16))
        k_even = (k_nat, k_swp)
        k_odd = (k_swp, k_nat)
        q_all = q_ref[pl.ds(r0, BLOCK), :]
        for pair in range(N_Q_HEADS // 2):
            grp = pair // (Q_PER_KV // 2)
            qp = q_all[:, pair * 2 * HEAD_DIM:(pair + 1) * 2 * HEAD_DIM]
            zero = jnp.zeros_like(qp)
            res = []
            esink = []
            for odd in range(2):
                head = 2 * pair + odd
                qh = jnp.where(lo, zero, qp) if odd else jnp.where(lo, qp, zero)
                kk = (k_odd if odd else k_even)[grp]
                vv = (v_odd if odd else v_even)[grp]
                s = _dot_nt(qh, kk) + bias_ref[first, head]
                sink = sink_ref[head]
                m = jnp.maximum(jnp.max(s, axis=-1, keepdims=True), sink)
                p = jnp.exp(s - m).astype(BF16)
                res.append(_dot(p, vv))
                esink.append(jnp.exp(sink - m))
            num = jnp.where(lo, res[0], res[1])
            den = pltpu.roll(jnp.where(lo, res[1], res[0]), HEAD_DIM, 1)
            den = den + jnp.where(lo, esink[0], esink[1])
            o_ref[pl.ds(r0, BLOCK), pair * 2 * HEAD_DIM:(pair + 1) * 2 * HEAD_DIM] = (
                num / den).astype(BF16)
        return carry

    lax.fori_loop(0, ATTN_SUB, sub_block, 0)


def _attention(q, kv, sinks, bias):
    bsz, seq, _ = q.shape
    nq = seq // ATTN_TQ
    return pl.pallas_call(
        _attn_kernel,
        grid=(bsz, nq),
        in_specs=[
            pl.BlockSpec(memory_space=pltpu.SMEM),
            pl.BlockSpec((None, ATTN_TQ, ATTN_WIDTH), lambda b, n: (b, n, 0)),
            pl.BlockSpec((None, ATTN_TQ, 2 * KV_WIDTH), lambda b, n: (b, n, 0)),
            pl.BlockSpec((None, BLOCK, 2 * KV_WIDTH),
                         lambda b, n: (b, jnp.maximum(n * ATTN_SUB - 1, 0), 0)),
            _const_spec((2, N_Q_HEADS, BLOCK, 2 * BLOCK)),
        ],
        out_specs=pl.BlockSpec((None, ATTN_TQ, ATTN_WIDTH), lambda b, n: (b, n, 0)),
        out_shape=jax.ShapeDtypeStruct((bsz, seq, ATTN_WIDTH), BF16),
        scratch_shapes=[pltpu.VMEM((ATTN_TQ + BLOCK, 2 * KV_WIDTH), BF16)],
        compiler_params=pltpu.CompilerParams(
            dimension_semantics=("arbitrary", "arbitrary"), vmem_limit_bytes=32 * 1024 * 1024),
        name="attn",
    )(sinks, q, kv, kv, bias)


def _scan_permutation():
    p = np.zeros((SCAN_ROWS, SCAN_ROWS), np.float32)
    for c in range(SCAN_CHUNKS):
        for s in range(SCAN_LEN):
            p[s * SCAN_CHUNKS + c, c * SCAN_LEN + s] = 1.0
    return p


def _ssm_kernel(u_ref, perm_ref, permt_ref, wb_ref, wc_ref, a_ref, apow_ref, d_ref,
                wglu_ref, bglu_ref, o_ref, carry_scr, bu_scr, xs_scr, x0_scr, y_scr):
    @pl.when(pl.program_id(1) == 0)
    def _():
        carry_scr[...] = jnp.zeros_like(carry_scr)

    u = u_ref[...]
    u_hi = u.astype(BF16)
    u_lo = (u - u_hi.astype(F32)).astype(BF16)
    perm = perm_ref[...]
    up_hi = _dot(perm, u_hi)
    u_perm = up_hi + _dot(perm, u_lo)
    ub = up_hi.astype(BF16)

    def recurrence(a_re, a_im, store):
        def step(s, state):
            xr, xi = state
            rows = pl.ds(pl.multiple_of(s * SCAN_CHUNKS, SCAN_CHUNKS), SCAN_CHUNKS)
            br = bu_scr[rows, :LG_STATES]
            bi = bu_scr[rows, LG_STATES:]
            nr = a_re * xr - a_im * xi + br
            ni = a_re * xi + a_im * xr + bi
            if store:
                xs_scr[rows, :LG_STATES] = nr
                xs_scr[rows, LG_STATES:] = ni
            return nr, ni
        return step

    for lg in range(SSM_LANE_GROUPS):
        bu_scr[...] = _dot(ub[:, lg * LG_CHANNELS:(lg + 1) * LG_CHANNELS], wb_ref[lg])
        a_re = jnp.broadcast_to(a_ref[lg, 0:1, :], (SCAN_CHUNKS, LG_STATES))
        a_im = jnp.broadcast_to(a_ref[lg, 1:2, :], (SCAN_CHUNKS, LG_STATES))
        zeros = jnp.zeros((SCAN_CHUNKS, LG_STATES), F32)
        e_re, e_im = lax.fori_loop(0, SCAN_LEN, recurrence(a_re, a_im, False),
                                   (zeros, zeros), unroll=4)
        p_re = apow_ref[lg, 0:1, :]
        p_im = apow_ref[lg, 1:2, :]
        cur_re = carry_scr[lg, 0:1, :]
        cur_im = carry_scr[lg, 1:2, :]
        for c in range(SCAN_CHUNKS):
            x0_scr[0, c:c + 1, :] = cur_re
            x0_scr[1, c:c + 1, :] = cur_im
            nxt_re = p_re * cur_re - p_im * cur_im + e_re[c:c + 1, :]
            nxt_im = p_re * cur_im + p_im * cur_re + e_im[c:c + 1, :]
            cur_re, cur_im = nxt_re, nxt_im
        carry_scr[lg, 0:1, :] = cur_re
        carry_scr[lg, 1:2, :] = cur_im
        lax.fori_loop(0, SCAN_LEN, recurrence(a_re, a_im, True),
                      (x0_scr[0], x0_scr[1]), unroll=4)
        y_scr[:, lg * LG_CHANNELS:(lg + 1) * LG_CHANNELS] = _dot(
            xs_scr[...].astype(BF16), wc_ref[lg])

    y = y_scr[...] + d_ref[...] * u_perm
    z = _dot(_gelu(y).astype(BF16), wglu_ref[...]) + bglu_ref[...]
    ssm = z[:, :SSM_WIDTH] * jax.nn.sigmoid(z[:, SSM_WIDTH:])
    o_ref[...] = _dot(permt_ref[...], ssm.astype(BF16)).astype(BF16)


def _ssm(u, perm, permt, wb, wc, a, apow, d, wglu, bglu):
    bsz, seq, _ = u.shape
    return pl.pallas_call(
        _ssm_kernel,
        grid=(bsz, seq // SCAN_ROWS),
        in_specs=[
            pl.BlockSpec((None, SCAN_ROWS, SSM_WIDTH), lambda b, s: (b, s, 0)),
            _const_spec((SCAN_ROWS, SCAN_ROWS)),
            _const_spec((SCAN_ROWS, SCAN_ROWS)),
            _const_spec((SSM_LANE_GROUPS, LG_CHANNELS, 2 * LG_STATES)),
            _const_spec((SSM_LANE_GROUPS, 2 * LG_STATES, LG_CHANNELS)),
            _const_spec((SSM_LANE_GROUPS, 2, LG_STATES)),
            _const_spec((SSM_LANE_GROUPS, 2, LG_STATES)),
            _const_spec((1, SSM_WIDTH)),
            _const_spec((SSM_WIDTH, 2 * SSM_WIDTH)),
            _const_spec((1, 2 * SSM_WIDTH)),
        ],
        out_specs=pl.BlockSpec((None, SCAN_ROWS, SSM_WIDTH), lambda b, s: (b, s, 0)),
        out_shape=jax.ShapeDtypeStruct((bsz, seq, SSM_WIDTH), BF16),
        scratch_shapes=[
            pltpu.VMEM((SSM_LANE_GROUPS, 2, LG_STATES), F32),
            pltpu.VMEM((SCAN_ROWS, 2 * LG_STATES), F32),
            pltpu.VMEM((SCAN_ROWS, 2 * LG_STATES), F32),
            pltpu.VMEM((2, SCAN_CHUNKS, LG_STATES), F32),
            pltpu.VMEM((SCAN_ROWS, SSM_WIDTH), F32),
        ],
        compiler_params=pltpu.CompilerParams(
            dimension_semantics=("arbitrary", "arbitrary"), vmem_limit_bytes=40 * 1024 * 1024),
        name="ssm",
    )(u, perm, permt, wb, wc, a, apow, d, wglu, bglu)


def _ssm_params(a_re, a_im, log_dt, b_re, b_im, c_re, c_im):
    dt = jnp.exp(log_dt)[:, None]
    mag = jnp.exp(a_re * dt)
    ab_re = mag * jnp.cos(a_im * dt)
    ab_im = mag * jnp.sin(a_im * dt)
    nr = ab_re - 1.0
    ni = ab_im
    den = a_re * a_re + a_im * a_im
    z_re = (nr * a_re + ni * a_im) / den
    z_im = (ni * a_re - nr * a_im) / den
    bb_re = z_re[..., None] * b_re - z_im[..., None] * b_im
    bb_im = z_re[..., None] * b_im + z_im[..., None] * b_re

    gl = SSM_GROUPS // SSM_LANE_GROUPS
    eye = jnp.eye(gl, dtype=F32)

    def in_mat(bb):
        bb = bb.reshape(SSM_LANE_GROUPS, gl, SSM_STATE, SSM_GROUP)
        m = jnp.einsum('lgph,gk->lghkp', bb, eye)
        return m.reshape(SSM_LANE_GROUPS, LG_CHANNELS, LG_STATES)

    def out_mat(cc):
        cc = cc.reshape(SSM_LANE_GROUPS, gl, SSM_GROUP, SSM_STATE)
        m = jnp.einsum('lghp,gk->lgpkh', cc, eye)
        return m.reshape(SSM_LANE_GROUPS, LG_STATES, LG_CHANNELS)

    wb = jnp.concatenate([in_mat(bb_re), in_mat(bb_im)], axis=-1).astype(BF16)
    wc = jnp.concatenate([out_mat(c_re), out_mat(-c_im)], axis=-2).astype(BF16)

    pr, pi = ab_re, ab_im
    for _ in range(int(math.log2(SCAN_LEN))):
        pr, pi = pr * pr - pi * pi, 2.0 * pr * pi
    lanes = lambda v: v.reshape(SSM_LANE_GROUPS, LG_STATES)
    a = jnp.stack([lanes(ab_re), lanes(ab_im)], axis=1)
    apow = jnp.stack([lanes(pr), lanes(pi)], axis=1)
    return wb, wc, a, apow


def _merge_kernel(attn_ref, ssm_ref, ga_ref, gs_ref, x_ref, wba_ref, wbs_ref, wout_ref,
                  g_ref, x1_ref, h2_ref):
    a = _dot(attn_ref[...], wba_ref[...])
    s = _dot(ssm_ref[...], wbs_ref[...])
    merged = jax.nn.sigmoid(ga_ref[...]) * a + jax.nn.sigmoid(gs_ref[...]) * s
    x1 = x_ref[...] + _dot(merged.astype(BF16), wout_ref[...])
    x1_ref[...] = x1
    h2_ref[...] = _rms_norm(x1, g_ref[...]).astype(BF16)


def _merge(attn, ssm, ga, gs, x2, wba, wbs, wout, g, tm):
    t = x2.shape[0]
    row = lambda width: pl.BlockSpec((tm, width), lambda i: (i, 0))
    return pl.pallas_call(
        _merge_kernel,
        grid=(t // tm,),
        in_specs=[row(ATTN_WIDTH), row(SSM_WIDTH), row(D_MODEL), row(D_MODEL), row(D_MODEL),
                  _const_spec((ATTN_WIDTH, D_MODEL)), _const_spec((SSM_WIDTH, D_MODEL)),
                  _const_spec((D_MODEL, D_MODEL)), _const_spec((1, D_MODEL))],
        out_specs=[row(D_MODEL), row(D_MODEL)],
        out_shape=[jax.ShapeDtypeStruct((t, D_MODEL), F32),
                   jax.ShapeDtypeStruct((t, D_MODEL), BF16)],
        compiler_params=pltpu.CompilerParams(
            dimension_semantics=("arbitrary",), vmem_limit_bytes=56 * 1024 * 1024),
        name="merge",
    )(attn, ssm, ga, gs, x2, wba, wbs, wout, g)


FFN_HALO = 16


def _ffn_kernel(h_ref, halo_ref, x1_ref, wv_ref, wg_ref, cw_ref, cb_ref, wd_ref, gf_ref,
                o_ref, acc_scr, gbuf, *, tm, seq):
    i = pl.program_id(0)
    j = pl.program_id(1)
    h = h_ref[...]
    val = _dot(h, wv_ref[...])
    gate = _dot(h, wg_ref[...])
    seq_start = (i * tm) % seq == 0
    halo = _dot(halo_ref[...], wg_ref[...])
    gbuf[0:FFN_HALO, :] = jnp.where(seq_start, 0.0, halo)
    gbuf[FFN_HALO:, :] = gate
    conv = cb_ref[...] + cw_ref[2:3, :] * gate
    conv = conv + cw_ref[1:2, :] * gbuf[FFN_HALO - 1:FFN_HALO - 1 + tm, :]
    conv = conv + cw_ref[0:1, :] * gbuf[FFN_HALO - 2:FFN_HALO - 2 + tm, :]
    act = (val * _gelu(conv)).astype(BF16)
    part = _dot(act, wd_ref[...])

    @pl.when(j == 0)
    def _():
        acc_scr[...] = x1_ref[...] + part

    @pl.when(j > 0)
    def _():
        acc_scr[...] += part

    @pl.when(j == pl.num_programs(1) - 1)
    def _():
        o_ref[...] = _rms_norm(acc_scr[...], gf_ref[...])


def _ffn(h2, x1, w_up, conv_w, conv_b, w_down, gf, seq, tm, tf):
    t = h2.shape[0]
    nf = D_FF // tf
    kern = functools.partial(_ffn_kernel, tm=tm, seq=seq)
    return pl.pallas_call(
        kern,
        grid=(t // tm, nf),
        in_specs=[
            pl.BlockSpec((tm, D_MODEL), lambda i, j: (i, 0)),
            pl.BlockSpec((FFN_HALO, D_MODEL),
                         lambda i, j: (jnp.maximum(i * (tm // FFN_HALO) - 1, 0), 0)),
            pl.BlockSpec((tm, D_MODEL), lambda i, j: (i, 0)),
            pl.BlockSpec((D_MODEL, tf), lambda i, j: (0, j)),
            pl.BlockSpec((D_MODEL, tf), lambda i, j: (0, j + nf)),
            pl.BlockSpec((CONV_WIDTH, tf), lambda i, j: (0, j)),
            pl.BlockSpec((1, tf), lambda i, j: (0, j)),
            pl.BlockSpec((tf, D_MODEL), lambda i, j: (j, 0)),
            _const_spec((1, D_MODEL)),
        ],
        out_specs=pl.BlockSpec((tm, D_MODEL), lambda i, j: (i, 0)),
        out_shape=jax.ShapeDtypeStruct((t, D_MODEL), F32),
        scratch_shapes=[pltpu.VMEM((tm, D_MODEL), F32),
                        pltpu.VMEM((tm + FFN_HALO, tf), F32)],
        compiler_params=pltpu.CompilerParams(
            dimension_semantics=("arbitrary", "arbitrary"), vmem_limit_bytes=56 * 1024 * 1024),
        name="ffn",
    )(h2, h2, x1, w_up, w_up, conv_w, conv_b, w_down, gf)


def _layer(x, attn_norm_g, w_in, b_in, attn_sinks, ssm_a_re, ssm_a_im, ssm_log_dt, ssm_b_re,
           ssm_b_im, ssm_c_re, ssm_c_im, ssm_d, w_glu, b_glu, w_branch_attn, w_branch_ssm,
           w_out, ffn_norm_g, w_up, conv_w, conv_b, w_down, out_norm_g):
    bsz, seq, _ = x.shape
    t = bsz * seq
    x2 = x.reshape(t, D_MODEL)
    row = lambda v: v.reshape(1, -1)

    q, kv, u, ga, gs = _in_proj(x2, row(attn_norm_g), w_in.astype(BF16), row(b_in), tm=256)

    attn = _attention(q.reshape(bsz, seq, ATTN_WIDTH), kv.reshape(bsz, seq, 2 * KV_WIDTH),
                      attn_sinks, _attn_bias_table())

    wb, wc, a, apow = _ssm_params(ssm_a_re, ssm_a_im, ssm_log_dt, ssm_b_re, ssm_b_im,
                                  ssm_c_re, ssm_c_im)
    perm = _scan_permutation()
    ssm = _ssm(u.reshape(bsz, seq, SSM_WIDTH), jnp.asarray(perm, BF16), jnp.asarray(perm.T, BF16),
               wb, wc, a, apow, row(ssm_d), w_glu.astype(BF16), row(b_glu))

    x1, h2 = _merge(attn.reshape(t, ATTN_WIDTH), ssm.reshape(t, SSM_WIDTH), ga, gs, x2,
                    w_branch_attn.astype(BF16), w_branch_ssm.astype(BF16), w_out.astype(BF16),
                    row(ffn_norm_g), tm=256)

    out = _ffn(h2, x1, w_up.astype(BF16), conv_w, row(conv_b), w_down.astype(BF16),
               row(out_norm_g), seq=seq, tm=512, tf=512)
    return out.reshape(bsz, seq, D_MODEL)


def kernel(x, attn_norm_g, w_in, b_in, attn_sinks, ssm_a_re, ssm_a_im, ssm_log_dt, ssm_b_re,
           ssm_b_im, ssm_c_re, ssm_c_im, ssm_d, w_glu, b_glu, w_branch_attn, w_branch_ssm, w_out,
           ffn_norm_g, w_up, conv_w, conv_b, w_down, final_norm_g):
    depth = w_in.shape[0]
    assert depth == 1, "the fused final norm assumes a single layer"
    return _layer(x, attn_norm_g[0], w_in[0], b_in[0], attn_sinks[0], ssm_a_re[0], ssm_a_im[0],
                  ssm_log_dt[0], ssm_b_re[0], ssm_b_im[0], ssm_c_re[0], ssm_c_im[0], ssm_d[0],
                  w_glu[0], b_glu[0], w_branch_attn[0], w_branch_ssm[0], w_out[0], ffn_norm_g[0],
                  w_up[0], conv_w[0], conv_b[0], w_down[0], final_norm_g)
```

```python
import functools
import math

import jax
import jax.numpy as jnp
import numpy as np
from jax import lax
from jax.experimental import pallas as pl
from jax.experimental.pallas import tpu as pltpu

D_MODEL = 2048
N_Q_HEADS = 16
N_KV_HEADS = 2
HEAD_DIM = 64
Q_PER_KV = N_Q_HEADS // N_KV_HEADS
WINDOW = 128
BLOCK = 128
ATTN_WIDTH = N_Q_HEADS * HEAD_DIM
KV_WIDTH = N_KV_HEADS * HEAD_DIM
SSM_GROUP = 16
SSM_GROUPS = 32
SSM_WIDTH = SSM_GROUP * SSM_GROUPS
SSM_STATE = 64
D_FF = 5632
CONV_WIDTH = 3
RMS_EPS = 1e-6
IN_COLS = ATTN_WIDTH + 2 * KV_WIDTH + SSM_WIDTH + 2 * D_MODEL
NEG_BIG = -1e30

V7X_LANES = 128
V7X_SUBLANES = 8
V7X_VMEM_BYTES = 64 * 1024 * 1024

BF16 = jnp.bfloat16
F32 = jnp.float32

_Q0 = 0
_K0 = ATTN_WIDTH
_V0 = _K0 + KV_WIDTH
_U0 = _V0 + KV_WIDTH
_GA0 = _U0 + SSM_WIDTH
_GS0 = _GA0 + D_MODEL

SCAN_CHUNKS = V7X_SUBLANES
SCAN_LEN = 64
SCAN_ROWS = SCAN_CHUNKS * SCAN_LEN
SSM_LANE_GROUPS = 4
LG_STATES = SSM_GROUPS * SSM_STATE // SSM_LANE_GROUPS
LG_CHANNELS = SSM_WIDTH // SSM_LANE_GROUPS


def _dot(a, b):
    return jnp.dot(a, b, preferred_element_type=F32)


def _dot_nt(a, b):
    return lax.dot_general(a, b, (((1,), (1,)), ((), ())), preferred_element_type=F32)


def _rms_norm(x, g):
    return x * lax.rsqrt(jnp.mean(x * x, axis=-1, keepdims=True) + RMS_EPS) * g


def _gelu(x):
    return 0.5 * x * (1.0 + lax.erf(x * np.float32(math.sqrt(0.5))))


def _const_spec(shape):
    nd = len(shape)
    return pl.BlockSpec(shape, lambda *_: (0,) * nd, pipeline_mode=pl.Buffered(1))


def _in_proj_kernel(x_ref, g_ref, w_ref, b_ref, q_ref, kv_ref, u_ref, ga_ref, gs_ref):
    h = _rms_norm(x_ref[...], g_ref[...]).astype(BF16)
    proj = _dot(h, w_ref[...]) + b_ref[...]
    q_ref[...] = (proj[:, _Q0:_K0] * (HEAD_DIM ** -0.5)).astype(BF16)
    kv_ref[...] = proj[:, _K0:_U0].astype(BF16)
    u_ref[...] = proj[:, _U0:_GA0]
    ga_ref[...] = proj[:, _GA0:_GS0]
    gs_ref[...] = proj[:, _GS0:]


def _in_proj(x2, g, w, b, tm):
    t = x2.shape[0]
    row = lambda width: pl.BlockSpec((tm, width), lambda i: (i, 0))
    return pl.pallas_call(
        _in_proj_kernel,
        grid=(t // tm,),
        in_specs=[row(D_MODEL), _const_spec((1, D_MODEL)),
                  _const_spec((D_MODEL, IN_COLS)), _const_spec((1, IN_COLS))],
        out_specs=[row(ATTN_WIDTH), row(2 * KV_WIDTH), row(SSM_WIDTH),
                   row(D_MODEL), row(D_MODEL)],
        out_shape=[jax.ShapeDtypeStruct((t, ATTN_WIDTH), BF16),
                   jax.ShapeDtypeStruct((t, 2 * KV_WIDTH), BF16),
                   jax.ShapeDtypeStruct((t, SSM_WIDTH), F32),
                   jax.ShapeDtypeStruct((t, D_MODEL), F32),
                   jax.ShapeDtypeStruct((t, D_MODEL), F32)],
        compiler_params=pltpu.CompilerParams(
            dimension_semantics=("arbitrary",), vmem_limit_bytes=56 * 1024 * 1024),
        name="in_proj",
    )(x2, g, w, b)


ATTN_TQ = 512
ATTN_SUB = ATTN_TQ // BLOCK


def _attn_bias_table():
    q_idx = np.arange(BLOCK)[:, None]
    s_idx = np.arange(2 * BLOCK)[None, :]
    dist = q_idx + BLOCK - s_idx
    band = (dist >= 0) & (dist < WINDOW)
    slopes = 2.0 ** (-8.0 * np.arange(1, N_Q_HEADS + 1, dtype=np.float32) / N_Q_HEADS)
    alibi = -slopes[:, None, None].astype(np.float32) * dist.astype(np.float32)[None]
    general = np.where(band[None], alibi, np.float32(NEG_BIG))
    first = np.where((band & (s_idx >= BLOCK))[None], alibi, np.float32(NEG_BIG))
    return jnp.asarray(np.stack([general, first]).astype(np.float32))


def _attn_kernel(sink_ref, q_ref, kvc_ref, kvp_ref, bias_ref, o_ref, kbuf):
    n = pl.program_id(1)
    kbuf[0:BLOCK, :] = kvp_ref[...]
    kbuf[BLOCK:, :] = kvc_ref[...]
    lane = lax.broadcasted_iota(jnp.int32, (1, 2 * HEAD_DIM), 1)
    lo = lane < HEAD_DIM

    def sub_block(sb, carry):
        r0 = pl.multiple_of(sb * BLOCK, BLOCK)
        first = jnp.where(jnp.logical_and(n == 0, sb == 0), 1, 0)
        keys = kbuf[pl.ds(r0, 2 * BLOCK), :]
        k_nat = keys[:, :KV_WIDTH].astype(F32)
        v_nat = keys[:, KV_WIDTH:].astype(F32)
        k_swp = pltpu.roll(k_nat, HEAD_DIM, 1)
        v_swp = pltpu.roll(v_nat, HEAD_DIM, 1)
        k_nat = k_nat.astype(BF16)
        k_swp = k_swp.astype(BF16)
        v_even = (jnp.where(lo, v_nat, 1.0).astype(BF16), jnp.where(lo, v_swp, 1.0).astype(BF16))
        v_odd = (jnp.where(lo, 1.0, v_swp).astype(BF16), jnp.where(lo, 1.0, v_nat).astype(BF16))
        k_even = (k_nat, k_swp)
        k_odd = (k_swp, k_nat)
        q_all = q_ref[pl.ds(r0, BLOCK), :]
        for pair in range(N_Q_HEADS // 2):
            grp = pair // (Q_PER_KV // 2)
            qp = q_all[:, pair * 2 * HEAD_DIM:(pair + 1) * 2 * HEAD_DIM]
            zero = jnp.zeros_like(qp)
            res = []
            esink = []
            for odd in range(2):
                head = 2 * pair + odd
                qh = jnp.where(lo, zero, qp) if odd else jnp.where(lo, qp, zero)
                kk = (k_odd if odd else k_even)[grp]
                vv = (v_odd if odd else v_even)[grp]
                s = _dot_nt(qh, kk) + bias_ref[first, head]
                sink = sink_ref[head]
                m = jnp.maximum(jnp.max(s, axis=-1, keepdims=True), sink)
                p = jnp.exp(s - m).astype(BF16)
                res.append(_dot(p, vv))
                esink.append(jnp.exp(sink - m))
            num = jnp.where(lo, res[0], res[1])
            den = pltpu.roll(jnp.where(lo, res[1], res[0]), HEAD_DIM, 1)
            den = den + jnp.where(lo, esink[0], esink[1])
            o_ref[pl.ds(r0, BLOCK), pair * 2 * HEAD_DIM:(pair + 1) * 2 * HEAD_DIM] = (
                num / den).astype(BF16)
        return carry

    lax.fori_loop(0, ATTN_SUB, sub_block, 0)


def _attention(q, kv, sinks, bias):
    bsz, seq, _ = q.shape
    nq = seq // ATTN_TQ
    return pl.pallas_call(
        _attn_kernel,
        grid=(bsz, nq),
        in_specs=[
            pl.BlockSpec(memory_space=pltpu.SMEM),
            pl.BlockSpec((None, ATTN_TQ, ATTN_WIDTH), lambda b, n: (b, n, 0)),
            pl.BlockSpec((None, ATTN_TQ, 2 * KV_WIDTH), lambda b, n: (b, n, 0)),
            pl.BlockSpec((None, BLOCK, 2 * KV_WIDTH),
                         lambda b, n: (b, jnp.maximum(n * ATTN_SUB - 1, 0), 0)),
            _const_spec((2, N_Q_HEADS, BLOCK, 2 * BLOCK)),
        ],
        out_specs=pl.BlockSpec((None, ATTN_TQ, ATTN_WIDTH), lambda b, n: (b, n, 0)),
        out_shape=jax.ShapeDtypeStruct((bsz, seq, ATTN_WIDTH), BF16),
        scratch_shapes=[pltpu.VMEM((ATTN_TQ + BLOCK, 2 * KV_WIDTH), BF16)],
        compiler_params=pltpu.CompilerParams(
            dimension_semantics=("arbitrary", "arbitrary"), vmem_limit_bytes=32 * 1024 * 1024),
        name="attn",
    )(sinks, q, kv, kv, bias)


def _scan_permutation():
    p = np.zeros((SCAN_ROWS, SCAN_ROWS), np.float32)
    for c in range(SCAN_CHUNKS):
        for s in range(SCAN_LEN):
            p[s * SCAN_CHUNKS + c, c * SCAN_LEN + s] = 1.0
    return p


def _ssm_kernel(u_ref, perm_ref, permt_ref, wb_ref, wc_ref, a_ref, apow_ref, d_ref,
                wglu_ref, bglu_ref, o_ref, carry_scr, bu_scr, xs_scr, x0_scr, y_scr):
    @pl.when(pl.program_id(1) == 0)
    def _():
        carry_scr[...] = jnp.zeros_like(carry_scr)

    u = u_ref[...]
    u_hi = u.astype(BF16)
    u_lo = (u - u_hi.astype(F32)).astype(BF16)
    perm = perm_ref[...]
    up_hi = _dot(perm, u_hi)
    u_perm = up_hi + _dot(perm, u_lo)
    ub = up_hi.astype(BF16)

    def recurrence(lg, a_re, a_im, state, store):
        xr, xi = state
        for s in range(SCAN_LEN):
            rows = slice(s * SCAN_CHUNKS, (s + 1) * SCAN_CHUNKS)
            br = bu_scr[lg, rows, :LG_STATES]
            bi = bu_scr[lg, rows, LG_STATES:]
            xr, xi = a_re * xr - a_im * xi + br, a_re * xi + a_im * xr + bi
            if store:
                xs_scr[lg, rows, :LG_STATES] = xr
                xs_scr[lg, rows, LG_STATES:] = xi
        return xr, xi

    for lg in range(SSM_LANE_GROUPS):
        bu_scr[lg] = _dot(ub[:, lg * LG_CHANNELS:(lg + 1) * LG_CHANNELS], wb_ref[lg])
        a_re = jnp.broadcast_to(a_ref[lg, 0:1, :], (SCAN_CHUNKS, LG_STATES))
        a_im = jnp.broadcast_to(a_ref[lg, 1:2, :], (SCAN_CHUNKS, LG_STATES))
        zeros = jnp.zeros((SCAN_CHUNKS, LG_STATES), F32)
        e_re, e_im = recurrence(lg, a_re, a_im, (zeros, zeros), False)
        p_re = apow_ref[lg, 0:1, :]
        p_im = apow_ref[lg, 1:2, :]
        cur_re = carry_scr[lg, 0:1, :]
        cur_im = carry_scr[lg, 1:2, :]
        for c in range(SCAN_CHUNKS):
            x0_scr[lg, 0, c:c + 1, :] = cur_re
            x0_scr[lg, 1, c:c + 1, :] = cur_im
            nxt_re = p_re * cur_re - p_im * cur_im + e_re[c:c + 1, :]
            nxt_im = p_re * cur_im + p_im * cur_re + e_im[c:c + 1, :]
            cur_re, cur_im = nxt_re, nxt_im
        carry_scr[lg, 0:1, :] = cur_re
        carry_scr[lg, 1:2, :] = cur_im
        recurrence(lg, a_re, a_im, (x0_scr[lg, 0], x0_scr[lg, 1]), True)
        y_scr[:, lg * LG_CHANNELS:(lg + 1) * LG_CHANNELS] = _dot(
            xs_scr[lg].astype(BF16), wc_ref[lg])

    y = y_scr[...] + d_ref[...] * u_perm
    z = _dot(_gelu(y).astype(BF16), wglu_ref[...]) + bglu_ref[...]
    ssm = z[:, :SSM_WIDTH] * jax.nn.sigmoid(z[:, SSM_WIDTH:])
    o_ref[...] = _dot(permt_ref[...], ssm.astype(BF16)).astype(BF16)


def _ssm(u, perm, permt, wb, wc, a, apow, d, wglu, bglu):
    bsz, seq, _ = u.shape
    return pl.pallas_call(
        _ssm_kernel,
        grid=(bsz, seq // SCAN_ROWS),
        in_specs=[
            pl.BlockSpec((None, SCAN_ROWS, SSM_WIDTH), lambda b, s: (b, s, 0)),
            _const_spec((SCAN_ROWS, SCAN_ROWS)),
            _const_spec((SCAN_ROWS, SCAN_ROWS)),
            _const_spec((SSM_LANE_GROUPS, LG_CHANNELS, 2 * LG_STATES)),
            _const_spec((SSM_LANE_GROUPS, 2 * LG_STATES, LG_CHANNELS)),
            _const_spec((SSM_LANE_GROUPS, 2, LG_STATES)),
            _const_spec((SSM_LANE_GROUPS, 2, LG_STATES)),
            _const_spec((1, SSM_WIDTH)),
            _const_spec((SSM_WIDTH, 2 * SSM_WIDTH)),
            _const_spec((1, 2 * SSM_WIDTH)),
        ],
        out_specs=pl.BlockSpec((None, SCAN_ROWS, SSM_WIDTH), lambda b, s: (b, s, 0)),
        out_shape=jax.ShapeDtypeStruct((bsz, seq, SSM_WIDTH), BF16),
        scratch_shapes=[
            pltpu.VMEM((SSM_LANE_GROUPS, 2, LG_STATES), F32),
            pltpu.VMEM((SSM_LANE_GROUPS, SCAN_ROWS, 2 * LG_STATES), F32),
            pltpu.VMEM((SSM_LANE_GROUPS, SCAN_ROWS, 2 * LG_STATES), F32),
            pltpu.VMEM((SSM_LANE_GROUPS, 2, SCAN_CHUNKS, LG_STATES), F32),
            pltpu.VMEM((SCAN_ROWS, SSM_WIDTH), F32),
        ],
        compiler_params=pltpu.CompilerParams(
            dimension_semantics=("arbitrary", "arbitrary"), vmem_limit_bytes=40 * 1024 * 1024),
        name="ssm",
    )(u, perm, permt, wb, wc, a, apow, d, wglu, bglu)


def _ssm_params(a_re, a_im, log_dt, b_re, b_im, c_re, c_im):
    dt = jnp.exp(log_dt)[:, None]
    mag = jnp.exp(a_re * dt)
    ab_re = mag * jnp.cos(a_im * dt)
    ab_im = mag * jnp.sin(a_im * dt)
    nr = ab_re - 1.0
    ni = ab_im
    den = a_re * a_re + a_im * a_im
    z_re = (nr * a_re + ni * a_im) / den
    z_im = (ni * a_re - nr * a_im) / den
    bb_re = z_re[..., None] * b_re - z_im[..., None] * b_im
    bb_im = z_re[..., None] * b_im + z_im[..., None] * b_re

    gl = SSM_GROUPS // SSM_LANE_GROUPS
    eye = jnp.eye(gl, dtype=F32)

    def in_mat(bb):
        bb = bb.reshape(SSM_LANE_GROUPS, gl, SSM_STATE, SSM_GROUP)
        m = jnp.einsum('lgph,gk->lghkp', bb, eye)
        return m.reshape(SSM_LANE_GROUPS, LG_CHANNELS, LG_STATES)

    def out_mat(cc):
        cc = cc.reshape(SSM_LANE_GROUPS, gl, SSM_GROUP, SSM_STATE)
        m = jnp.einsum('lghp,gk->lgpkh', cc, eye)
        return m.reshape(SSM_LANE_GROUPS, LG_STATES, LG_CHANNELS)

    wb = jnp.concatenate([in_mat(bb_re), in_mat(bb_im)], axis=-1).astype(BF16)
    wc = jnp.concatenate([out_mat(c_re), out_mat(-c_im)], axis=-2).astype(BF16)

    pr, pi = ab_re, ab_im
    for _ in range(int(math.log2(SCAN_LEN))):
        pr, pi = pr * pr - pi * pi, 2.0 * pr * pi
    lanes = lambda v: v.reshape(SSM_LANE_GROUPS, LG_STATES)
    a = jnp.stack([lanes(ab_re), lanes(ab_im)], axis=1)
    apow = jnp.stack([lanes(pr), lanes(pi)], axis=1)
    return wb, wc, a, apow


def _merge_kernel(attn_ref, ssm_ref, ga_ref, gs_ref, x_ref, wba_ref, wbs_ref, wout_ref,
                  g_ref, x1_ref, h2_ref):
    a = _dot(attn_ref[...], wba_ref[...])
    s = _dot(ssm_ref[...], wbs_ref[...])
    merged = jax.nn.sigmoid(ga_ref[...]) * a + jax.nn.sigmoid(gs_ref[...]) * s
    x1 = x_ref[...] + _dot(merged.astype(BF16), wout_ref[...])
    x1_ref[...] = x1
    h2_ref[...] = _rms_norm(x1, g_ref[...]).astype(BF16)


def _merge(attn, ssm, ga, gs, x2, wba, wbs, wout, g, tm):
    t = x2.shape[0]
    row = lambda width: pl.BlockSpec((tm, width), lambda i: (i, 0))
    return pl.pallas_call(
        _merge_kernel,
        grid=(t // tm,),
        in_specs=[row(ATTN_WIDTH), row(SSM_WIDTH), row(D_MODEL), row(D_MODEL), row(D_MODEL),
                  _const_spec((ATTN_WIDTH, D_MODEL)), _const_spec((SSM_WIDTH, D_MODEL)),
                  _const_spec((D_MODEL, D_MODEL)), _const_spec((1, D_MODEL))],
        out_specs=[row(D_MODEL), row(D_MODEL)],
        out_shape=[jax.ShapeDtypeStruct((t, D_MODEL), F32),
                   jax.ShapeDtypeStruct((t, D_MODEL), BF16)],
        compiler_params=pltpu.CompilerParams(
            dimension_semantics=("arbitrary",), vmem_limit_bytes=56 * 1024 * 1024),
        name="merge",
    )(attn, ssm, ga, gs, x2, wba, wbs, wout, g)


FFN_TAIL = V7X_SUBLANES
FFN_CHUNK = 256


def _ffn_kernel(h_ref, x1_ref, wv_ref, wg_ref, cw_ref, cb_ref, wd_ref, gf_ref,
                o_ref, acc_scr, act_scr, tail_scr, *, tm, tf, seq):
    i = pl.program_id(0)
    j = pl.program_id(1)

    @pl.when(jnp.logical_and(i == 0, j == 0))
    def _():
        tail_scr[...] = jnp.zeros_like(tail_scr)

    @pl.when(j == 0)
    def _():
        acc_scr[...] = x1_ref[...]

    seq_start = (i * tm) % seq == 0
    h = h_ref[...]
    row = lax.broadcasted_iota(jnp.int32, (FFN_TAIL, FFN_CHUNK), 0)
    for c in range(tf // FFN_CHUNK):
        cols = slice(c * FFN_CHUNK, (c + 1) * FFN_CHUNK)
        val = _dot(h, wv_ref[:, cols])
        gate = _dot(h, wg_ref[:, cols])
        prev = jnp.where(seq_start, 0.0, tail_scr[j, :, cols])
        tail_scr[j, :, cols] = gate[tm - FFN_TAIL:, :]
        conv = cb_ref[:, cols] + cw_ref[CONV_WIDTH - 1:CONV_WIDTH, cols] * gate
        for back in range(1, CONV_WIDTH):
            shifted = pltpu.roll(gate, back, 0)
            top = jnp.where(row < back, pltpu.roll(prev, back, 0), shifted[:FFN_TAIL])
            shifted = jnp.concatenate([top, shifted[FFN_TAIL:]], axis=0)
            k = CONV_WIDTH - 1 - back
            conv = conv + cw_ref[k:k + 1, cols] * shifted
        act_scr[:, cols] = (val * _gelu(conv)).astype(BF16)
    acc_scr[...] += _dot(act_scr[...], wd_ref[...])

    @pl.when(j == pl.num_programs(1) - 1)
    def _():
        o_ref[...] = _rms_norm(acc_scr[...], gf_ref[...])


def _ffn(h2, x1, w_up, conv_w, conv_b, w_down, gf, seq, tm, tf):
    t = h2.shape[0]
    nf = D_FF // tf
    kern = functools.partial(_ffn_kernel, tm=tm, tf=tf, seq=seq)
    return pl.pallas_call(
        kern,
        grid=(t // tm, nf),
        in_specs=[
            pl.BlockSpec((tm, D_MODEL), lambda i, j: (i, 0)),
            pl.BlockSpec((tm, D_MODEL), lambda i, j: (i, 0)),
            pl.BlockSpec((D_MODEL, tf), lambda i, j: (0, j)),
            pl.BlockSpec((D_MODEL, tf), lambda i, j: (0, j + nf)),
            pl.BlockSpec((CONV_WIDTH, tf), lambda i, j: (0, j)),
            pl.BlockSpec((1, tf), lambda i, j: (0, j)),
            pl.BlockSpec((tf, D_MODEL), lambda i, j: (j, 0)),
            _const_spec((1, D_MODEL)),
        ],
        out_specs=pl.BlockSpec((tm, D_MODEL), lambda i, j: (i, 0)),
        out_shape=jax.ShapeDtypeStruct((t, D_MODEL), F32),
        scratch_shapes=[pltpu.VMEM((tm, D_MODEL), F32),
                        pltpu.VMEM((tm, tf), BF16),
                        pltpu.VMEM((nf, FFN_TAIL, tf), F32)],
        compiler_params=pltpu.CompilerParams(
            dimension_semantics=("arbitrary", "arbitrary"), vmem_limit_bytes=56 * 1024 * 1024),
        name="ffn",
    )(h2, x1, w_up, w_up, conv_w, conv_b, w_down, gf)


def _layer(x, attn_norm_g, w_in, b_in, attn_sinks, ssm_a_re, ssm_a_im, ssm_log_dt, ssm_b_re,
           ssm_b_im, ssm_c_re, ssm_c_im, ssm_d, w_glu, b_glu, w_branch_attn, w_branch_ssm,
           w_out, ffn_norm_g, w_up, conv_w, conv_b, w_down, out_norm_g):
    bsz, seq, _ = x.shape
    t = bsz * seq
    x2 = x.reshape(t, D_MODEL)
    row = lambda v: v.reshape(1, -1)

    q, kv, u, ga, gs = _in_proj(x2, row(attn_norm_g), w_in.astype(BF16), row(b_in), tm=256)

    attn = _attention(q.reshape(bsz, seq, ATTN_WIDTH), kv.reshape(bsz, seq, 2 * KV_WIDTH),
                      attn_sinks, _attn_bias_table())

    wb, wc, a, apow = _ssm_params(ssm_a_re, ssm_a_im, ssm_log_dt, ssm_b_re, ssm_b_im,
                                  ssm_c_re, ssm_c_im)
    perm = _scan_permutation()
    ssm = _ssm(u.reshape(bsz, seq, SSM_WIDTH), jnp.asarray(perm, BF16), jnp.asarray(perm.T, BF16),
               wb, wc, a, apow, row(ssm_d), w_glu.astype(BF16), row(b_glu))

    x1, h2 = _merge(attn.reshape(t, ATTN_WIDTH), ssm.reshape(t, SSM_WIDTH), ga, gs, x2,
                    w_branch_attn.astype(BF16), w_branch_ssm.astype(BF16), w_out.astype(BF16),
                    row(ffn_norm_g), tm=256)

    out = _ffn(h2, x1, w_up.astype(BF16), conv_w, row(conv_b), w_down.astype(BF16),
               row(out_norm_g), seq=seq, tm=512, tf=512)
    return out.reshape(bsz, seq, D_MODEL)


def kernel(x, attn_norm_g, w_in, b_in, attn_sinks, ssm_a_re, ssm_a_im, ssm_log_dt, ssm_b_re,
           ssm_b_im, ssm_c_re, ssm_c_im, ssm_d, w_glu, b_glu, w_branch_attn, w_branch_ssm, w_out,
           ffn_norm_g, w_up, conv_w, conv_b, w_down, final_norm_g):
    depth = w_in.shape[0]
    assert depth == 1, "the fused final norm assumes a single layer"
    return _layer(x, attn_norm_g[0], w_in[0], b_in[0], attn_sinks[0], ssm_a_re[0], ssm_a_im[0],
                  ssm_log_dt[0], ssm_b_re[0], ssm_b_im[0], ssm_c_re[0], ssm_c_im[0], ssm_d[0],
                  w_glu[0], b_glu[0], w_branch_attn[0], w_branch_ssm[0], w_out[0], ffn_norm_g[0],
                  w_up[0], conv_w[0], conv_b[0], w_down[0], final_norm_g)
```

```python
import functools
import math

import jax
import jax.numpy as jnp
import numpy as np
from jax import lax
from jax.experimental import pallas as pl
from jax.experimental.pallas import tpu as pltpu

D_MODEL = 2048
N_Q_HEADS = 16
N_KV_HEADS = 2
HEAD_DIM = 64
Q_PER_KV = N_Q_HEADS // N_KV_HEADS
WINDOW = 128
BLOCK = 128
ATTN_WIDTH = N_Q_HEADS * HEAD_DIM
KV_WIDTH = N_KV_HEADS * HEAD_DIM
SSM_GROUP = 16
SSM_GROUPS = 32
SSM_WIDTH = SSM_GROUP * SSM_GROUPS
SSM_STATE = 64
D_FF = 5632
CONV_WIDTH = 3
RMS_EPS = 1e-6
IN_COLS = ATTN_WIDTH + 2 * KV_WIDTH + SSM_WIDTH + 2 * D_MODEL
NEG_BIG = -1e30

V7X_LANES = 128
V7X_SUBLANES = 8
V7X_VMEM_BYTES = 64 * 1024 * 1024

BF16 = jnp.bfloat16
F32 = jnp.float32

_Q0 = 0
_K0 = ATTN_WIDTH
_V0 = _K0 + KV_WIDTH
_U0 = _V0 + KV_WIDTH
_GA0 = _U0 + SSM_WIDTH
_GS0 = _GA0 + D_MODEL

SCAN_CHUNKS = V7X_SUBLANES
SCAN_LEN = 64
SCAN_ROWS = SCAN_CHUNKS * SCAN_LEN
SSM_LANE_GROUPS = 4
LG_STATES = SSM_GROUPS * SSM_STATE // SSM_LANE_GROUPS
LG_CHANNELS = SSM_WIDTH // SSM_LANE_GROUPS


def _dot(a, b):
    return jnp.dot(a, b, preferred_element_type=F32)


def _dot_nt(a, b):
    return lax.dot_general(a, b, (((1,), (1,)), ((), ())), preferred_element_type=F32)


def _rms_norm(x, g):
    return x * lax.rsqrt(jnp.mean(x * x, axis=-1, keepdims=True) + RMS_EPS) * g


def _gelu(x):
    return 0.5 * x * (1.0 + lax.erf(x * np.float32(math.sqrt(0.5))))


def _const_spec(shape):
    nd = len(shape)
    return pl.BlockSpec(shape, lambda *_: (0,) * nd, pipeline_mode=pl.Buffered(1))


def _in_proj_kernel(x_ref, g_ref, w_ref, b_ref, q_ref, kv_ref, u_ref, ga_ref, gs_ref):
    h = _rms_norm(x_ref[...], g_ref[...]).astype(BF16)
    proj = _dot(h, w_ref[...]) + b_ref[...]
    q_ref[...] = (proj[:, _Q0:_K0] * (HEAD_DIM ** -0.5)).astype(BF16)
    kv_ref[...] = proj[:, _K0:_U0].astype(BF16)
    u_ref[...] = proj[:, _U0:_GA0]
    ga_ref[...] = proj[:, _GA0:_GS0]
    gs_ref[...] = proj[:, _GS0:]


def _in_proj(x2, g, w, b, tm):
    t = x2.shape[0]
    row = lambda width: pl.BlockSpec((tm, width), lambda i: (i, 0))
    return pl.pallas_call(
        _in_proj_kernel,
        grid=(t // tm,),
        in_specs=[row(D_MODEL), _const_spec((1, D_MODEL)),
                  _const_spec((D_MODEL, IN_COLS)), _const_spec((1, IN_COLS))],
        out_specs=[row(ATTN_WIDTH), row(2 * KV_WIDTH), row(SSM_WIDTH),
                   row(D_MODEL), row(D_MODEL)],
        out_shape=[jax.ShapeDtypeStruct((t, ATTN_WIDTH), BF16),
                   jax.ShapeDtypeStruct((t, 2 * KV_WIDTH), BF16),
                   jax.ShapeDtypeStruct((t, SSM_WIDTH), F32),
                   jax.ShapeDtypeStruct((t, D_MODEL), F32),
                   jax.ShapeDtypeStruct((t, D_MODEL), F32)],
        compiler_params=pltpu.CompilerParams(
            dimension_semantics=("arbitrary",), vmem_limit_bytes=56 * 1024 * 1024),
        name="in_proj",
    )(x2, g, w, b)


ATTN_TQ = 1024
ATTN_SUB = ATTN_TQ // BLOCK


def _attn_bias_table(sinks):
    q_idx = np.arange(BLOCK)[:, None]
    s_idx = np.arange(2 * BLOCK)[None, :]
    dist = q_idx + BLOCK - s_idx
    band = (dist >= 0) & (dist < WINDOW)
    assert not band[:, 0].any()
    slopes = 2.0 ** (-8.0 * np.arange(1, N_Q_HEADS + 1, dtype=np.float32) / N_Q_HEADS)
    alibi = -slopes[:, None, None].astype(np.float32) * dist.astype(np.float32)[None]
    general = np.where(band[None], alibi, np.float32(NEG_BIG))
    first = np.where((band & (s_idx >= BLOCK))[None], alibi, np.float32(NEG_BIG))
    table = jnp.asarray(np.stack([general, first]).astype(np.float32))
    sink_col = jnp.asarray((s_idx == 0)[None, None])
    return jnp.where(sink_col, sinks.astype(F32)[None, :, None, None], table)


ATTN_PATCH = 16


def _attn_kernel(q_ref, kvc_ref, kvp_ref, bias_ref, o_ref, ops_scr):
    n = pl.program_id(1)
    lane = lax.broadcasted_iota(jnp.int32, (1, 2 * HEAD_DIM), 1)
    lo = lane < HEAD_DIM

    def operands(kv):
        kv = kv.astype(F32)
        k_nat = kv[:, :KV_WIDTH]
        v_nat = kv[:, KV_WIDTH:]
        k_swp = pltpu.roll(k_nat, HEAD_DIM, 1)
        v_swp = pltpu.roll(v_nat, HEAD_DIM, 1)
        return (jnp.where(lo, k_nat, 0.0), jnp.where(lo, k_swp, 0.0),
                jnp.where(lo, 0.0, k_swp), jnp.where(lo, 0.0, k_nat),
                jnp.where(lo, v_nat, 1.0), jnp.where(lo, v_swp, 1.0),
                jnp.where(lo, 1.0, v_swp), jnp.where(lo, 1.0, v_nat))

    for idx, op in enumerate(operands(kvp_ref[...])):
        ops_scr[idx, 0:BLOCK, :] = op.astype(BF16)
    for idx, op in enumerate(operands(kvc_ref[...])):
        ops_scr[idx, BLOCK:, :] = op.astype(BF16)

    sink_row = lax.broadcasted_iota(jnp.int32, (ATTN_PATCH, 1), 0) == 0
    zero = jnp.zeros((1, 2 * HEAD_DIM), F32)
    sink_fill = (zero, zero, zero, zero,
                 jnp.where(lo, 0.0, 1.0), jnp.where(lo, 0.0, 1.0),
                 jnp.where(lo, 1.0, 0.0), jnp.where(lo, 1.0, 0.0))

    def window(idx, r0):
        top = ops_scr[idx, pl.ds(r0, ATTN_PATCH), :].astype(F32)
        top = jnp.where(sink_row, sink_fill[idx], top).astype(BF16)
        rest = ops_scr[idx, pl.ds(r0 + ATTN_PATCH, 2 * BLOCK - ATTN_PATCH), :]
        return jnp.concatenate([top, rest], axis=0)

    def sub_block(sb, carry):
        r0 = pl.multiple_of(sb * BLOCK, BLOCK)
        first = jnp.where(jnp.logical_and(n == 0, sb == 0), 1, 0)
        win = [window(idx, r0) for idx in range(8)]
        q_all = q_ref[pl.ds(r0, BLOCK), :]
        for pair in range(N_Q_HEADS // 2):
            grp = pair // (Q_PER_KV // 2)
            qp = q_all[:, pair * 2 * HEAD_DIM:(pair + 1) * 2 * HEAD_DIM]
            res = []
            for odd in range(2):
                head = 2 * pair + odd
                kk = win[2 * odd + grp]
                vv = win[4 + 2 * odd + grp]
                s = _dot_nt(qp, kk) + bias_ref[first, head]
                m = jnp.max(s, axis=-1, keepdims=True)
                p = jnp.exp(s - m).astype(BF16)
                res.append(_dot(p, vv))
            num = jnp.where(lo, res[0], res[1])
            den = pltpu.roll(jnp.where(lo, res[1], res[0]), HEAD_DIM, 1)
            o_ref[pl.ds(r0, BLOCK), pair * 2 * HEAD_DIM:(pair + 1) * 2 * HEAD_DIM] = (
                num / den).astype(BF16)
        return carry

    lax.fori_loop(0, ATTN_SUB, sub_block, 0)


def _attention(q, kv, bias):
    bsz, seq, _ = q.shape
    nq = seq // ATTN_TQ
    return pl.pallas_call(
        _attn_kernel,
        grid=(bsz, nq),
        in_specs=[
            pl.BlockSpec((None, ATTN_TQ, ATTN_WIDTH), lambda b, n: (b, n, 0)),
            pl.BlockSpec((None, ATTN_TQ, 2 * KV_WIDTH), lambda b, n: (b, n, 0)),
            pl.BlockSpec((None, BLOCK, 2 * KV_WIDTH),
                         lambda b, n: (b, jnp.maximum(n * ATTN_SUB - 1, 0), 0)),
            _const_spec((2, N_Q_HEADS, BLOCK, 2 * BLOCK)),
        ],
        out_specs=pl.BlockSpec((None, ATTN_TQ, ATTN_WIDTH), lambda b, n: (b, n, 0)),
        out_shape=jax.ShapeDtypeStruct((bsz, seq, ATTN_WIDTH), BF16),
        scratch_shapes=[pltpu.VMEM((8, ATTN_TQ + BLOCK, 2 * HEAD_DIM), BF16)],
        compiler_params=pltpu.CompilerParams(
            dimension_semantics=("arbitrary", "arbitrary"), vmem_limit_bytes=32 * 1024 * 1024),
        name="attn",
    )(q, kv, kv, bias)


def _scan_permutation():
    p = np.zeros((SCAN_ROWS, SCAN_ROWS), np.float32)
    for c in range(SCAN_CHUNKS):
        for s in range(SCAN_LEN):
            p[s * SCAN_CHUNKS + c, c * SCAN_LEN + s] = 1.0
    return p


def _ssm_kernel(u_ref, perm_ref, permt_ref, wb_ref, wc_ref, a_ref, apow_ref, d_ref,
                wglu_ref, bglu_ref, o_ref, carry_scr, bu_scr, xs_scr, x0_scr, y_scr):
    @pl.when(pl.program_id(1) == 0)
    def _():
        carry_scr[...] = jnp.zeros_like(carry_scr)

    u = u_ref[...]
    u_hi = u.astype(BF16)
    u_lo = (u - u_hi.astype(F32)).astype(BF16)
    perm = perm_ref[...]
    up_hi = _dot(perm, u_hi)
    u_perm = up_hi + _dot(perm, u_lo)
    ub = up_hi.astype(BF16)

    def recurrence(lg, a_re, a_im, state, store):
        xr, xi = state
        for s in range(SCAN_LEN):
            rows = slice(s * SCAN_CHUNKS, (s + 1) * SCAN_CHUNKS)
            br = bu_scr[lg, rows, :LG_STATES]
            bi = bu_scr[lg, rows, LG_STATES:]
            xr, xi = a_re * xr - a_im * xi + br, a_re * xi + a_im * xr + bi
            if store:
                xs_scr[lg, rows, :LG_STATES] = xr
                xs_scr[lg, rows, LG_STATES:] = xi
        return xr, xi

    for lg in range(SSM_LANE_GROUPS):
        bu_scr[lg] = _dot(ub[:, lg * LG_CHANNELS:(lg + 1) * LG_CHANNELS], wb_ref[lg])
        a_re = jnp.broadcast_to(a_ref[lg, 0:1, :], (SCAN_CHUNKS, LG_STATES))
        a_im = jnp.broadcast_to(a_ref[lg, 1:2, :], (SCAN_CHUNKS, LG_STATES))
        zeros = jnp.zeros((SCAN_CHUNKS, LG_STATES), F32)
        e_re, e_im = recurrence(lg, a_re, a_im, (zeros, zeros), False)
        p_re = apow_ref[lg, 0:1, :]
        p_im = apow_ref[lg, 1:2, :]
        cur_re = carry_scr[lg, 0:1, :]
        cur_im = carry_scr[lg, 1:2, :]
        for c in range(SCAN_CHUNKS):
            x0_scr[lg, 0, c:c + 1, :] = cur_re
            x0_scr[lg, 1, c:c + 1, :] = cur_im
            nxt_re = p_re * cur_re - p_im * cur_im + e_re[c:c + 1, :]
            nxt_im = p_re * cur_im + p_im * cur_re + e_im[c:c + 1, :]
            cur_re, cur_im = nxt_re, nxt_im
        carry_scr[lg, 0:1, :] = cur_re
        carry_scr[lg, 1:2, :] = cur_im
        recurrence(lg, a_re, a_im, (x0_scr[lg, 0], x0_scr[lg, 1]), True)
        y_scr[:, lg * LG_CHANNELS:(lg + 1) * LG_CHANNELS] = _dot(
            xs_scr[lg].astype(BF16), wc_ref[lg])

    y = y_scr[...] + d_ref[...] * u_perm
    z = _dot(_gelu(y).astype(BF16), wglu_ref[...]) + bglu_ref[...]
    ssm = z[:, :SSM_WIDTH] * jax.nn.sigmoid(z[:, SSM_WIDTH:])
    o_ref[...] = _dot(permt_ref[...], ssm.astype(BF16)).astype(BF16)


def _ssm(u, perm, permt, wb, wc, a, apow, d, wglu, bglu):
    bsz, seq, _ = u.shape
    return pl.pallas_call(
        _ssm_kernel,
        grid=(bsz, seq // SCAN_ROWS),
        in_specs=[
            pl.BlockSpec((None, SCAN_ROWS, SSM_WIDTH), lambda b, s: (b, s, 0)),
            _const_spec((SCAN_ROWS, SCAN_ROWS)),
            _const_spec((SCAN_ROWS, SCAN_ROWS)),
            _const_spec((SSM_LANE_GROUPS, LG_CHANNELS, 2 * LG_STATES)),
            _const_spec((SSM_LANE_GROUPS, 2 * LG_STATES, LG_CHANNELS)),
            _const_spec((SSM_LANE_GROUPS, 2, LG_STATES)),
            _const_spec((SSM_LANE_GROUPS, 2, LG_STATES)),
            _const_spec((1, SSM_WIDTH)),
            _const_spec((SSM_WIDTH, 2 * SSM_WIDTH)),
            _const_spec((1, 2 * SSM_WIDTH)),
        ],
        out_specs=pl.BlockSpec((None, SCAN_ROWS, SSM_WIDTH), lambda b, s: (b, s, 0)),
        out_shape=jax.ShapeDtypeStruct((bsz, seq, SSM_WIDTH), BF16),
        scratch_shapes=[
            pltpu.VMEM((SSM_LANE_GROUPS, 2, LG_STATES), F32),
            pltpu.VMEM((SSM_LANE_GROUPS, SCAN_ROWS, 2 * LG_STATES), F32),
            pltpu.VMEM((SSM_LANE_GROUPS, SCAN_ROWS, 2 * LG_STATES), F32),
            pltpu.VMEM((SSM_LANE_GROUPS, 2, SCAN_CHUNKS, LG_STATES), F32),
            pltpu.VMEM((SCAN_ROWS, SSM_WIDTH), F32),
        ],
        compiler_params=pltpu.CompilerParams(
            dimension_semantics=("arbitrary", "arbitrary"), vmem_limit_bytes=40 * 1024 * 1024),
        name="ssm",
    )(u, perm, permt, wb, wc, a, apow, d, wglu, bglu)


def _ssm_params(a_re, a_im, log_dt, b_re, b_im, c_re, c_im):
    dt = jnp.exp(log_dt)[:, None]
    mag = jnp.exp(a_re * dt)
    ab_re = mag * jnp.cos(a_im * dt)
    ab_im = mag * jnp.sin(a_im * dt)
    nr = ab_re - 1.0
    ni = ab_im
    den = a_re * a_re + a_im * a_im
    z_re = (nr * a_re + ni * a_im) / den
    z_im = (ni * a_re - nr * a_im) / den
    bb_re = z_re[..., None] * b_re - z_im[..., None] * b_im
    bb_im = z_re[..., None] * b_im + z_im[..., None] * b_re

    gl = SSM_GROUPS // SSM_LANE_GROUPS
    eye = jnp.eye(gl, dtype=F32)

    def in_mat(bb):
        bb = bb.reshape(SSM_LANE_GROUPS, gl, SSM_STATE, SSM_GROUP)
        m = jnp.einsum('lgph,gk->lghkp', bb, eye)
        return m.reshape(SSM_LANE_GROUPS, LG_CHANNELS, LG_STATES)

    def out_mat(cc):
        cc = cc.reshape(SSM_LANE_GROUPS, gl, SSM_GROUP, SSM_STATE)
        m = jnp.einsum('lghp,gk->lgpkh', cc, eye)
        return m.reshape(SSM_LANE_GROUPS, LG_STATES, LG_CHANNELS)

    wb = jnp.concatenate([in_mat(bb_re), in_mat(bb_im)], axis=-1).astype(BF16)
    wc = jnp.concatenate([out_mat(c_re), out_mat(-c_im)], axis=-2).astype(BF16)

    pr, pi = ab_re, ab_im
    for _ in range(int(math.log2(SCAN_LEN))):
        pr, pi = pr * pr - pi * pi, 2.0 * pr * pi
    lanes = lambda v: v.reshape(SSM_LANE_GROUPS, LG_STATES)
    a = jnp.stack([lanes(ab_re), lanes(ab_im)], axis=1)
    apow = jnp.stack([lanes(pr), lanes(pi)], axis=1)
    return wb, wc, a, apow


def _merge_kernel(attn_ref, ssm_ref, ga_ref, gs_ref, x_ref, wba_ref, wbs_ref, wout_ref,
                  g_ref, x1_ref, h2_ref):
    a = _dot(attn_ref[...], wba_ref[...])
    s = _dot(ssm_ref[...], wbs_ref[...])
    merged = jax.nn.sigmoid(ga_ref[...]) * a + jax.nn.sigmoid(gs_ref[...]) * s
    x1 = x_ref[...] + _dot(merged.astype(BF16), wout_ref[...])
    x1_ref[...] = x1
    h2_ref[...] = _rms_norm(x1, g_ref[...]).astype(BF16)


def _merge(attn, ssm, ga, gs, x2, wba, wbs, wout, g, tm):
    t = x2.shape[0]
    row = lambda width: pl.BlockSpec((tm, width), lambda i: (i, 0))
    return pl.pallas_call(
        _merge_kernel,
        grid=(t // tm,),
        in_specs=[row(ATTN_WIDTH), row(SSM_WIDTH), row(D_MODEL), row(D_MODEL), row(D_MODEL),
                  _const_spec((ATTN_WIDTH, D_MODEL)), _const_spec((SSM_WIDTH, D_MODEL)),
                  _const_spec((D_MODEL, D_MODEL)), _const_spec((1, D_MODEL))],
        out_specs=[row(D_MODEL), row(D_MODEL)],
        out_shape=[jax.ShapeDtypeStruct((t, D_MODEL), F32),
                   jax.ShapeDtypeStruct((t, D_MODEL), BF16)],
        compiler_params=pltpu.CompilerParams(
            dimension_semantics=("arbitrary",), vmem_limit_bytes=56 * 1024 * 1024),
        name="merge",
    )(attn, ssm, ga, gs, x2, wba, wbs, wout, g)


FFN_TAIL = V7X_SUBLANES
FFN_CHUNK = 256


def _ffn_kernel(h_ref, x1_hbm, wv_ref, wg_ref, cw_ref, cb_ref, wd_ref, gf_ref,
                o_ref, x1_scr, act_scr, tail_scr, x1_sem, *, tm, tf, seq):
    i = pl.program_id(0)
    j = pl.program_id(1)
    last = pl.num_programs(1) - 1

    def x1_copy():
        return pltpu.make_async_copy(x1_hbm.at[pl.ds(i * tm, tm), :], x1_scr, x1_sem)

    @pl.when(jnp.logical_and(i == 0, j == 0))
    def _():
        tail_scr[...] = jnp.zeros_like(tail_scr)

    @pl.when(j == 0)
    def _():
        x1_copy().start()
        o_ref[...] = jnp.zeros_like(o_ref)

    seq_start = (i * tm) % seq == 0
    row = lax.broadcasted_iota(jnp.int32, (FFN_TAIL, FFN_CHUNK), 0)
    h = h_ref[...]
    for c in range(tf // FFN_CHUNK):
        cols = slice(c * FFN_CHUNK, (c + 1) * FFN_CHUNK)
        val = _dot(h, wv_ref[:, cols])
        gate = _dot(h, wg_ref[:, cols])
        prev = jnp.where(seq_start, 0.0, tail_scr[j, :, cols])
        tail_scr[j, :, cols] = gate[tm - FFN_TAIL:, :]
        conv = cb_ref[:, cols] + cw_ref[CONV_WIDTH - 1:CONV_WIDTH, cols] * gate
        for back in range(1, CONV_WIDTH):
            shifted = pltpu.roll(gate, back, 0)
            top = jnp.where(row < back, pltpu.roll(prev, back, 0), shifted[:FFN_TAIL])
            shifted = jnp.concatenate([top, shifted[FFN_TAIL:]], axis=0)
            k = CONV_WIDTH - 1 - back
            conv = conv + cw_ref[k:k + 1, cols] * shifted
        act_scr[:, cols] = (val * _gelu(conv)).astype(BF16)
    o_ref[...] += _dot(act_scr[...], wd_ref[...])

    @pl.when(j == last)
    def _():
        x1_copy().wait()
        o_ref[...] = _rms_norm(x1_scr[...] + o_ref[...], gf_ref[...])


def _ffn(h2, x1, w_up, conv_w, conv_b, w_down, gf, seq, tm, tf):
    t = h2.shape[0]
    nf = D_FF // tf
    kern = functools.partial(_ffn_kernel, tm=tm, tf=tf, seq=seq)
    return pl.pallas_call(
        kern,
        grid=(t // tm, nf),
        in_specs=[
            pl.BlockSpec((tm, D_MODEL), lambda i, j: (i, 0)),
            pl.BlockSpec(memory_space=pl.ANY),
            pl.BlockSpec((D_MODEL, tf), lambda i, j: (0, j)),
            pl.BlockSpec((D_MODEL, tf), lambda i, j: (0, j + nf)),
            pl.BlockSpec((CONV_WIDTH, tf), lambda i, j: (0, j)),
            pl.BlockSpec((1, tf), lambda i, j: (0, j)),
            pl.BlockSpec((tf, D_MODEL), lambda i, j: (j, 0)),
            _const_spec((1, D_MODEL)),
        ],
        out_specs=pl.BlockSpec((tm, D_MODEL), lambda i, j: (i, 0)),
        out_shape=jax.ShapeDtypeStruct((t, D_MODEL), F32),
        scratch_shapes=[pltpu.VMEM((tm, D_MODEL), F32),
                        pltpu.VMEM((tm, tf), BF16),
                        pltpu.VMEM((nf, FFN_TAIL, tf), F32),
                        pltpu.SemaphoreType.DMA(())],
        compiler_params=pltpu.CompilerParams(
            dimension_semantics=("arbitrary", "arbitrary"), vmem_limit_bytes=60 * 1024 * 1024),
        name="ffn",
    )(h2, x1, w_up, w_up, conv_w, conv_b, w_down, gf)


def _layer(x, attn_norm_g, w_in, b_in, attn_sinks, ssm_a_re, ssm_a_im, ssm_log_dt, ssm_b_re,
           ssm_b_im, ssm_c_re, ssm_c_im, ssm_d, w_glu, b_glu, w_branch_attn, w_branch_ssm,
           w_out, ffn_norm_g, w_up, conv_w, conv_b, w_down, out_norm_g):
    bsz, seq, _ = x.shape
    t = bsz * seq
    x2 = x.reshape(t, D_MODEL)
    row = lambda v: v.reshape(1, -1)

    q, kv, u, ga, gs = _in_proj(x2, row(attn_norm_g), w_in.astype(BF16), row(b_in), tm=256)

    attn = _attention(q.reshape(bsz, seq, ATTN_WIDTH), kv.reshape(bsz, seq, 2 * KV_WIDTH),
                      _attn_bias_table(attn_sinks))

    wb, wc, a, apow = _ssm_params(ssm_a_re, ssm_a_im, ssm_log_dt, ssm_b_re, ssm_b_im,
                                  ssm_c_re, ssm_c_im)
    perm = _scan_permutation()
    ssm = _ssm(u.reshape(bsz, seq, SSM_WIDTH), jnp.asarray(perm, BF16), jnp.asarray(perm.T, BF16),
               wb, wc, a, apow, row(ssm_d), w_glu.astype(BF16), row(b_glu))

    x1, h2 = _merge(attn.reshape(t, ATTN_WIDTH), ssm.reshape(t, SSM_WIDTH), ga, gs, x2,
                    w_branch_attn.astype(BF16), w_branch_ssm.astype(BF16), w_out.astype(BF16),
                    row(ffn_norm_g), tm=256)

    out = _ffn(h2, x1, w_up.astype(BF16), conv_w, row(conv_b), w_down.astype(BF16),
               row(out_norm_g), seq=seq, tm=1024, tf=512)
    return out.reshape(bsz, seq, D_MODEL)


def kernel(x, attn_norm_g, w_in, b_in, attn_sinks, ssm_a_re, ssm_a_im, ssm_log_dt, ssm_b_re,
           ssm_b_im, ssm_c_re, ssm_c_im, ssm_d, w_glu, b_glu, w_branch_attn, w_branch_ssm, w_out,
           ffn_norm_g, w_up, conv_w, conv_b, w_down, final_norm_g):
    depth = w_in.shape[0]
    assert depth == 1, "the fused final norm assumes a single layer"
    return _layer(x, attn_norm_g[0], w_in[0], b_in[0], attn_sinks[0], ssm_a_re[0], ssm_a_im[0],
                  ssm_log_dt[0], ssm_b_re[0], ssm_b_im[0], ssm_c_re[0], ssm_c_im[0], ssm_d[0],
                  w_glu[0], b_glu[0], w_branch_attn[0], w_branch_ssm[0], w_out[0], ffn_norm_g[0],
                  w_up[0], conv_w[0], conv_b[0], w_down[0], final_norm_g)
```

```python
import functools
import math

import jax
import jax.numpy as jnp
import numpy as np
from jax import lax
from jax.experimental import pallas as pl
from jax.experimental.pallas import tpu as pltpu

D_MODEL = 2048
N_Q_HEADS = 16
N_KV_HEADS = 2
HEAD_DIM = 64
Q_PER_KV = N_Q_HEADS // N_KV_HEADS
WINDOW = 128
BLOCK = 128
ATTN_WIDTH = N_Q_HEADS * HEAD_DIM
KV_WIDTH = N_KV_HEADS * HEAD_DIM
SSM_GROUP = 16
SSM_GROUPS = 32
SSM_WIDTH = SSM_GROUP * SSM_GROUPS
SSM_STATE = 64
D_FF = 5632
CONV_WIDTH = 3
RMS_EPS = 1e-6
IN_COLS = ATTN_WIDTH + 2 * KV_WIDTH + SSM_WIDTH + 2 * D_MODEL
NEG_BIG = -1e30

V7X_LANES = 128
V7X_SUBLANES = 8
V7X_VMEM_BYTES = 64 * 1024 * 1024

BF16 = jnp.bfloat16
F32 = jnp.float32

_Q0 = 0
_K0 = ATTN_WIDTH
_V0 = _K0 + KV_WIDTH
_U0 = _V0 + KV_WIDTH
_GA0 = _U0 + SSM_WIDTH
_GS0 = _GA0 + D_MODEL

SCAN_CHUNKS = V7X_SUBLANES
SCAN_LEN = 64
SCAN_ROWS = SCAN_CHUNKS * SCAN_LEN
SSM_LANE_GROUPS = 4
LG_STATES = SSM_GROUPS * SSM_STATE // SSM_LANE_GROUPS
LG_CHANNELS = SSM_WIDTH // SSM_LANE_GROUPS


def _dot(a, b):
    return jnp.dot(a, b, preferred_element_type=F32)


def _dot_nt(a, b):
    return lax.dot_general(a, b, (((1,), (1,)), ((), ())), preferred_element_type=F32)


def _rms_norm(x, g):
    return x * lax.rsqrt(jnp.mean(x * x, axis=-1, keepdims=True) + RMS_EPS) * g


def _gelu(x):
    return 0.5 * x * (1.0 + lax.erf(x * np.float32(math.sqrt(0.5))))


def _const_spec(shape):
    nd = len(shape)
    return pl.BlockSpec(shape, lambda *_: (0,) * nd, pipeline_mode=pl.Buffered(1))


N_PROJ_OUT = 5


def _in_proj_kernel(x_ref, g_ref, w_ref, b_ref, *refs):
    n_cast = (len(refs) - N_PROJ_OUT) // 2
    cast_in = refs[:n_cast]
    q_ref, kv_ref, u_ref, ga_ref, gs_ref = refs[n_cast:n_cast + N_PROJ_OUT]
    cast_out = refs[n_cast + N_PROJ_OUT:]
    h = _rms_norm(x_ref[...], g_ref[...]).astype(BF16)
    proj = _dot(h, w_ref[...]) + b_ref[...]
    q_ref[...] = (proj[:, _Q0:_K0] * (HEAD_DIM ** -0.5)).astype(BF16)
    kv_ref[...] = proj[:, _K0:_U0].astype(BF16)
    u_ref[...] = proj[:, _U0:_GA0]
    ga_ref[...] = proj[:, _GA0:_GS0]
    gs_ref[...] = proj[:, _GS0:]
    for src, dst in zip(cast_in, cast_out):
        dst[...] = src[...].astype(BF16)


def _cast_view(w, n_steps):
    bf16_rows = 2 * V7X_SUBLANES
    for cols in (w.shape[-1], 1408, 1024, 512):
        rows = w.size // (n_steps * cols)
        if w.size == n_steps * rows * cols and rows % bf16_rows == 0 and cols % V7X_LANES == 0:
            return w.reshape(n_steps * rows, cols), rows
    raise ValueError(f"no cast tiling for {w.shape} over {n_steps} steps")


def _in_proj(x2, g, w, b, cast_weights, tm):
    t = x2.shape[0]
    n_steps = t // tm
    row = lambda width: pl.BlockSpec((tm, width), lambda i: (i, 0))
    views = [_cast_view(cw, n_steps) for cw in cast_weights]
    cast_specs = [pl.BlockSpec((rows, v.shape[1]), lambda i: (i, 0)) for v, rows in views]
    outs = pl.pallas_call(
        _in_proj_kernel,
        grid=(n_steps,),
        in_specs=[row(D_MODEL), _const_spec((1, D_MODEL)),
                  _const_spec((D_MODEL, IN_COLS)), _const_spec((1, IN_COLS))] + cast_specs,
        out_specs=[row(ATTN_WIDTH), row(2 * KV_WIDTH), row(SSM_WIDTH),
                   row(D_MODEL), row(D_MODEL)] + cast_specs,
        out_shape=[jax.ShapeDtypeStruct((t, ATTN_WIDTH), BF16),
                   jax.ShapeDtypeStruct((t, 2 * KV_WIDTH), BF16),
                   jax.ShapeDtypeStruct((t, SSM_WIDTH), F32),
                   jax.ShapeDtypeStruct((t, D_MODEL), F32),
                   jax.ShapeDtypeStruct((t, D_MODEL), F32)]
        + [jax.ShapeDtypeStruct(v.shape, BF16) for v, _ in views],
        compiler_params=pltpu.CompilerParams(
            dimension_semantics=("arbitrary",), vmem_limit_bytes=58 * 1024 * 1024),
        name="in_proj",
    )(x2, g, w, b, *[v for v, _ in views])
    casts = [o.reshape(cw.shape) for o, cw in zip(outs[N_PROJ_OUT:], cast_weights)]
    return outs[:N_PROJ_OUT], casts


ATTN_TQ = 1024
ATTN_SUB = ATTN_TQ // BLOCK


def _attn_bias_table(sinks):
    q_idx = np.arange(BLOCK)[:, None]
    s_idx = np.arange(2 * BLOCK)[None, :]
    dist = q_idx + BLOCK - s_idx
    band = (dist >= 0) & (dist < WINDOW)
    assert not band[:, 0].any()
    slopes = 2.0 ** (-8.0 * np.arange(1, N_Q_HEADS + 1, dtype=np.float32) / N_Q_HEADS)
    alibi = -slopes[:, None, None].astype(np.float32) * dist.astype(np.float32)[None]
    general = np.where(band[None], alibi, np.float32(NEG_BIG))
    first = np.where((band & (s_idx >= BLOCK))[None], alibi, np.float32(NEG_BIG))
    table = jnp.asarray(np.stack([general, first]).astype(np.float32))
    sink_col = jnp.asarray((s_idx == 0)[None, None])
    return jnp.where(sink_col, sinks.astype(F32)[None, :, None, None], table)


ATTN_PATCH = 16


def _attn_kernel(q_ref, kvc_ref, kvp_ref, bias_ref, o_ref, ops_scr):
    n = pl.program_id(1)
    lane = lax.broadcasted_iota(jnp.int32, (1, 2 * HEAD_DIM), 1)
    lo = lane < HEAD_DIM

    def operands(kv):
        kv = kv.astype(F32)
        k_nat = kv[:, :KV_WIDTH]
        v_nat = kv[:, KV_WIDTH:]
        k_swp = pltpu.roll(k_nat, HEAD_DIM, 1)
        v_swp = pltpu.roll(v_nat, HEAD_DIM, 1)
        return (jnp.where(lo, k_nat, 0.0), jnp.where(lo, k_swp, 0.0),
                jnp.where(lo, 0.0, k_swp), jnp.where(lo, 0.0, k_nat),
                jnp.where(lo, v_nat, 1.0), jnp.where(lo, v_swp, 1.0),
                jnp.where(lo, 1.0, v_swp), jnp.where(lo, 1.0, v_nat))

    for idx, op in enumerate(operands(kvp_ref[...])):
        ops_scr[idx, 0:BLOCK, :] = op.astype(BF16)
    for idx, op in enumerate(operands(kvc_ref[...])):
        ops_scr[idx, BLOCK:, :] = op.astype(BF16)

    sink_row = lax.broadcasted_iota(jnp.int32, (ATTN_PATCH, 1), 0) == 0
    zero = jnp.zeros((1, 2 * HEAD_DIM), F32)
    sink_fill = (zero, zero, zero, zero,
                 jnp.where(lo, 0.0, 1.0), jnp.where(lo, 0.0, 1.0),
                 jnp.where(lo, 1.0, 0.0), jnp.where(lo, 1.0, 0.0))

    def window(idx, r0):
        top = ops_scr[idx, pl.ds(r0, ATTN_PATCH), :].astype(F32)
        top = jnp.where(sink_row, sink_fill[idx], top).astype(BF16)
        rest = ops_scr[idx, pl.ds(r0 + ATTN_PATCH, 2 * BLOCK - ATTN_PATCH), :]
        return jnp.concatenate([top, rest], axis=0)

    def sub_block(sb, carry):
        r0 = pl.multiple_of(sb * BLOCK, BLOCK)
        first = jnp.where(jnp.logical_and(n == 0, sb == 0), 1, 0)
        win = [window(idx, r0) for idx in range(8)]
        q_all = q_ref[pl.ds(r0, BLOCK), :]
        for pair in range(N_Q_HEADS // 2):
            grp = pair // (Q_PER_KV // 2)
            qp = q_all[:, pair * 2 * HEAD_DIM:(pair + 1) * 2 * HEAD_DIM]
            res = []
            for odd in range(2):
                head = 2 * pair + odd
                kk = win[2 * odd + grp]
                vv = win[4 + 2 * odd + grp]
                s = _dot_nt(qp, kk) + bias_ref[first, head]
                m = jnp.max(s, axis=-1, keepdims=True)
                p = jnp.exp(s - m).astype(BF16)
                res.append(_dot(p, vv))
            num = jnp.where(lo, res[0], res[1])
            den = pltpu.roll(jnp.where(lo, res[1], res[0]), HEAD_DIM, 1)
            o_ref[pl.ds(r0, BLOCK), pair * 2 * HEAD_DIM:(pair + 1) * 2 * HEAD_DIM] = (
                num / den).astype(BF16)
        return carry

    lax.fori_loop(0, ATTN_SUB, sub_block, 0)


def _attention(q, kv, bias):
    bsz, seq, _ = q.shape
    nq = seq // ATTN_TQ
    return pl.pallas_call(
        _attn_kernel,
        grid=(bsz, nq),
        in_specs=[
            pl.BlockSpec((None, ATTN_TQ, ATTN_WIDTH), lambda b, n: (b, n, 0)),
            pl.BlockSpec((None, ATTN_TQ, 2 * KV_WIDTH), lambda b, n: (b, n, 0)),
            pl.BlockSpec((None, BLOCK, 2 * KV_WIDTH),
                         lambda b, n: (b, jnp.maximum(n * ATTN_SUB - 1, 0), 0)),
            _const_spec((2, N_Q_HEADS, BLOCK, 2 * BLOCK)),
        ],
        out_specs=pl.BlockSpec((None, ATTN_TQ, ATTN_WIDTH), lambda b, n: (b, n, 0)),
        out_shape=jax.ShapeDtypeStruct((bsz, seq, ATTN_WIDTH), BF16),
        scratch_shapes=[pltpu.VMEM((8, ATTN_TQ + BLOCK, 2 * HEAD_DIM), BF16)],
        compiler_params=pltpu.CompilerParams(
            dimension_semantics=("arbitrary", "arbitrary"), vmem_limit_bytes=32 * 1024 * 1024),
        name="attn",
    )(q, kv, kv, bias)


def _scan_permutation():
    p = np.zeros((SCAN_ROWS, SCAN_ROWS), np.float32)
    for c in range(SCAN_CHUNKS):
        for s in range(SCAN_LEN):
            p[s * SCAN_CHUNKS + c, c * SCAN_LEN + s] = 1.0
    return p


def _ssm_kernel(u_ref, perm_ref, permt_ref, wb_ref, wc_ref, a_ref, apow_ref, d_ref,
                wglu_ref, bglu_ref, o_ref, carry_scr, bu_scr, xs_scr, x0_scr, y_scr):
    @pl.when(pl.program_id(1) == 0)
    def _():
        carry_scr[...] = jnp.zeros_like(carry_scr)

    u = u_ref[...]
    u_hi = u.astype(BF16)
    u_lo = (u - u_hi.astype(F32)).astype(BF16)
    perm = perm_ref[...]
    up_hi = _dot(perm, u_hi)
    u_perm = up_hi + _dot(perm, u_lo)
    ub = up_hi.astype(BF16)

    def recurrence(lg, a_re, a_im, state, store):
        xr, xi = state
        for s in range(SCAN_LEN):
            rows = slice(s * SCAN_CHUNKS, (s + 1) * SCAN_CHUNKS)
            br = bu_scr[lg, rows, :LG_STATES]
            bi = bu_scr[lg, rows, LG_STATES:]
            xr, xi = a_re * xr - a_im * xi + br, a_re * xi + a_im * xr + bi
            if store:
                xs_scr[lg, rows, :LG_STATES] = xr
                xs_scr[lg, rows, LG_STATES:] = xi
        return xr, xi

    for lg in range(SSM_LANE_GROUPS):
        bu_scr[lg] = _dot(ub[:, lg * LG_CHANNELS:(lg + 1) * LG_CHANNELS], wb_ref[lg])
        a_re = jnp.broadcast_to(a_ref[lg, 0:1, :], (SCAN_CHUNKS, LG_STATES))
        a_im = jnp.broadcast_to(a_ref[lg, 1:2, :], (SCAN_CHUNKS, LG_STATES))
        zeros = jnp.zeros((SCAN_CHUNKS, LG_STATES), F32)
        e_re, e_im = recurrence(lg, a_re, a_im, (zeros, zeros), False)
        p_re = apow_ref[lg, 0:1, :]
        p_im = apow_ref[lg, 1:2, :]
        cur_re = carry_scr[lg, 0:1, :]
        cur_im = carry_scr[lg, 1:2, :]
        for c in range(SCAN_CHUNKS):
            x0_scr[lg, 0, c:c + 1, :] = cur_re
            x0_scr[lg, 1, c:c + 1, :] = cur_im
            nxt_re = p_re * cur_re - p_im * cur_im + e_re[c:c + 1, :]
            nxt_im = p_re * cur_im + p_im * cur_re + e_im[c:c + 1, :]
            cur_re, cur_im = nxt_re, nxt_im
        carry_scr[lg, 0:1, :] = cur_re
        carry_scr[lg, 1:2, :] = cur_im
        recurrence(lg, a_re, a_im, (x0_scr[lg, 0], x0_scr[lg, 1]), True)
        y_scr[:, lg * LG_CHANNELS:(lg + 1) * LG_CHANNELS] = _dot(
            xs_scr[lg].astype(BF16), wc_ref[lg])

    y = y_scr[...] + d_ref[...] * u_perm
    z = _dot(_gelu(y).astype(BF16), wglu_ref[...]) + bglu_ref[...]
    ssm = z[:, :SSM_WIDTH] * jax.nn.sigmoid(z[:, SSM_WIDTH:])
    o_ref[...] = _dot(permt_ref[...], ssm.astype(BF16)).astype(BF16)


def _ssm(u, perm, permt, wb, wc, a, apow, d, wglu, bglu):
    bsz, seq, _ = u.shape
    return pl.pallas_call(
        _ssm_kernel,
        grid=(bsz, seq // SCAN_ROWS),
        in_specs=[
            pl.BlockSpec((None, SCAN_ROWS, SSM_WIDTH), lambda b, s: (b, s, 0)),
            _const_spec((SCAN_ROWS, SCAN_ROWS)),
            _const_spec((SCAN_ROWS, SCAN_ROWS)),
            _const_spec((SSM_LANE_GROUPS, LG_CHANNELS, 2 * LG_STATES)),
            _const_spec((SSM_LANE_GROUPS, 2 * LG_STATES, LG_CHANNELS)),
            _const_spec((SSM_LANE_GROUPS, 2, LG_STATES)),
            _const_spec((SSM_LANE_GROUPS, 2, LG_STATES)),
            _const_spec((1, SSM_WIDTH)),
            _const_spec((SSM_WIDTH, 2 * SSM_WIDTH)),
            _const_spec((1, 2 * SSM_WIDTH)),
        ],
        out_specs=pl.BlockSpec((None, SCAN_ROWS, SSM_WIDTH), lambda b, s: (b, s, 0)),
        out_shape=jax.ShapeDtypeStruct((bsz, seq, SSM_WIDTH), BF16),
        scratch_shapes=[
            pltpu.VMEM((SSM_LANE_GROUPS, 2, LG_STATES), F32),
            pltpu.VMEM((SSM_LANE_GROUPS, SCAN_ROWS, 2 * LG_STATES), F32),
            pltpu.VMEM((SSM_LANE_GROUPS, SCAN_ROWS, 2 * LG_STATES), F32),
            pltpu.VMEM((SSM_LANE_GROUPS, 2, SCAN_CHUNKS, LG_STATES), F32),
            pltpu.VMEM((SCAN_ROWS, SSM_WIDTH), F32),
        ],
        compiler_params=pltpu.CompilerParams(
            dimension_semantics=("arbitrary", "arbitrary"), vmem_limit_bytes=40 * 1024 * 1024),
        name="ssm",
    )(u, perm, permt, wb, wc, a, apow, d, wglu, bglu)


def _ssm_params(a_re, a_im, log_dt, b_re, b_im, c_re, c_im):
    dt = jnp.exp(log_dt)[:, None]
    mag = jnp.exp(a_re * dt)
    ab_re = mag * jnp.cos(a_im * dt)
    ab_im = mag * jnp.sin(a_im * dt)
    nr = ab_re - 1.0
    ni = ab_im
    den = a_re * a_re + a_im * a_im
    z_re = (nr * a_re + ni * a_im) / den
    z_im = (ni * a_re - nr * a_im) / den
    bb_re = z_re[..., None] * b_re - z_im[..., None] * b_im
    bb_im = z_re[..., None] * b_im + z_im[..., None] * b_re

    gl = SSM_GROUPS // SSM_LANE_GROUPS
    eye = jnp.eye(gl, dtype=F32)

    def in_mat(bb):
        bb = bb.reshape(SSM_LANE_GROUPS, gl, SSM_STATE, SSM_GROUP)
        m = jnp.einsum('lgph,gk->lghkp', bb, eye)
        return m.reshape(SSM_LANE_GROUPS, LG_CHANNELS, LG_STATES)

    def out_mat(cc):
        cc = cc.reshape(SSM_LANE_GROUPS, gl, SSM_GROUP, SSM_STATE)
        m = jnp.einsum('lghp,gk->lgpkh', cc, eye)
        return m.reshape(SSM_LANE_GROUPS, LG_STATES, LG_CHANNELS)

    wb = jnp.concatenate([in_mat(bb_re), in_mat(bb_im)], axis=-1).astype(BF16)
    wc = jnp.concatenate([out_mat(c_re), out_mat(-c_im)], axis=-2).astype(BF16)

    pr, pi = ab_re, ab_im
    for _ in range(int(math.log2(SCAN_LEN))):
        pr, pi = pr * pr - pi * pi, 2.0 * pr * pi
    lanes = lambda v: v.reshape(SSM_LANE_GROUPS, LG_STATES)
    a = jnp.stack([lanes(ab_re), lanes(ab_im)], axis=1)
    apow = jnp.stack([lanes(pr), lanes(pi)], axis=1)
    return wb, wc, a, apow


def _merge_kernel(attn_ref, ssm_ref, ga_ref, gs_ref, x_ref, wba_ref, wbs_ref, wout_ref,
                  g_ref, x1_ref, h2_ref):
    a = _dot(attn_ref[...], wba_ref[...])
    s = _dot(ssm_ref[...], wbs_ref[...])
    merged = jax.nn.sigmoid(ga_ref[...]) * a + jax.nn.sigmoid(gs_ref[...]) * s
    x1 = x_ref[...] + _dot(merged.astype(BF16), wout_ref[...])
    x1_ref[...] = x1
    h2_ref[...] = _rms_norm(x1, g_ref[...]).astype(BF16)


def _merge(attn, ssm, ga, gs, x2, wba, wbs, wout, g, tm):
    t = x2.shape[0]
    row = lambda width: pl.BlockSpec((tm, width), lambda i: (i, 0))
    return pl.pallas_call(
        _merge_kernel,
        grid=(t // tm,),
        in_specs=[row(ATTN_WIDTH), row(SSM_WIDTH), row(D_MODEL), row(D_MODEL), row(D_MODEL),
                  _const_spec((ATTN_WIDTH, D_MODEL)), _const_spec((SSM_WIDTH, D_MODEL)),
                  _const_spec((D_MODEL, D_MODEL)), _const_spec((1, D_MODEL))],
        out_specs=[row(D_MODEL), row(D_MODEL)],
        out_shape=[jax.ShapeDtypeStruct((t, D_MODEL), F32),
                   jax.ShapeDtypeStruct((t, D_MODEL), BF16)],
        compiler_params=pltpu.CompilerParams(
            dimension_semantics=("arbitrary",), vmem_limit_bytes=56 * 1024 * 1024),
        name="merge",
    )(attn, ssm, ga, gs, x2, wba, wbs, wout, g)


FFN_TAIL = V7X_SUBLANES
FFN_CHUNK = 256


def _ffn_kernel(h_ref, x1_hbm, wv_ref, wg_ref, cw_ref, cb_ref, wd_ref, gf_ref,
                o_ref, x1_scr, act_scr, tail_scr, x1_sem, *, tm, tf, seq):
    i = pl.program_id(0)
    j = pl.program_id(1)
    last = pl.num_programs(1) - 1

    def x1_copy():
        return pltpu.make_async_copy(x1_hbm.at[pl.ds(i * tm, tm), :], x1_scr, x1_sem)

    @pl.when(jnp.logical_and(i == 0, j == 0))
    def _():
        tail_scr[...] = jnp.zeros_like(tail_scr)

    @pl.when(j == 0)
    def _():
        x1_copy().start()
        o_ref[...] = jnp.zeros_like(o_ref)

    seq_start = (i * tm) % seq == 0
    row = lax.broadcasted_iota(jnp.int32, (FFN_TAIL, FFN_CHUNK), 0)
    h = h_ref[...]
    for c in range(tf // FFN_CHUNK):
        cols = slice(c * FFN_CHUNK, (c + 1) * FFN_CHUNK)
        val = _dot(h, wv_ref[:, cols])
        gate = _dot(h, wg_ref[:, cols])
        prev = jnp.where(seq_start, 0.0, tail_scr[j, :, cols])
        tail_scr[j, :, cols] = gate[tm - FFN_TAIL:, :]
        conv = cb_ref[:, cols] + cw_ref[CONV_WIDTH - 1:CONV_WIDTH, cols] * gate
        for back in range(1, CONV_WIDTH):
            shifted = pltpu.roll(gate, back, 0)
            top = jnp.where(row < back, pltpu.roll(prev, back, 0), shifted[:FFN_TAIL])
            shifted = jnp.concatenate([top, shifted[FFN_TAIL:]], axis=0)
            k = CONV_WIDTH - 1 - back
            conv = conv + cw_ref[k:k + 1, cols] * shifted
        act_scr[:, cols] = (val * _gelu(conv)).astype(BF16)
    o_ref[...] += _dot(act_scr[...], wd_ref[...])

    @pl.when(j == last)
    def _():
        x1_copy().wait()
        o_ref[...] = _rms_norm(x1_scr[...] + o_ref[...], gf_ref[...])


def _ffn(h2, x1, w_up, conv_w, conv_b, w_down, gf, seq, tm, tf):
    t = h2.shape[0]
    nf = D_FF // tf
    kern = functools.partial(_ffn_kernel, tm=tm, tf=tf, seq=seq)
    return pl.pallas_call(
        kern,
        grid=(t // tm, nf),
        in_specs=[
            pl.BlockSpec((tm, D_MODEL), lambda i, j: (i, 0)),
            pl.BlockSpec(memory_space=pl.ANY),
            pl.BlockSpec((D_MODEL, tf), lambda i, j: (0, j)),
            pl.BlockSpec((D_MODEL, tf), lambda i, j: (0, j + nf)),
            pl.BlockSpec((CONV_WIDTH, tf), lambda i, j: (0, j)),
            pl.BlockSpec((1, tf), lambda i, j: (0, j)),
            pl.BlockSpec((tf, D_MODEL), lambda i, j: (j, 0)),
            _const_spec((1, D_MODEL)),
        ],
        out_specs=pl.BlockSpec((tm, D_MODEL), lambda i, j: (i, 0)),
        out_shape=jax.ShapeDtypeStruct((t, D_MODEL), F32),
        scratch_shapes=[pltpu.VMEM((tm, D_MODEL), F32),
                        pltpu.VMEM((tm, tf), BF16),
                        pltpu.VMEM((nf, FFN_TAIL, tf), F32),
                        pltpu.SemaphoreType.DMA(())],
        compiler_params=pltpu.CompilerParams(
            dimension_semantics=("arbitrary", "arbitrary"), vmem_limit_bytes=60 * 1024 * 1024),
        name="ffn",
    )(h2, x1, w_up, w_up, conv_w, conv_b, w_down, gf)


def _layer(x, attn_norm_g, w_in, b_in, attn_sinks, ssm_a_re, ssm_a_im, ssm_log_dt, ssm_b_re,
           ssm_b_im, ssm_c_re, ssm_c_im, ssm_d, w_glu, b_glu, w_branch_attn, w_branch_ssm,
           w_out, ffn_norm_g, w_up, conv_w, conv_b, w_down, out_norm_g):
    bsz, seq, _ = x.shape
    t = bsz * seq
    x2 = x.reshape(t, D_MODEL)
    row = lambda v: v.reshape(1, -1)

    (q, kv, u, ga, gs), (w_glu, w_branch_attn, w_branch_ssm, w_out, w_up, w_down) = _in_proj(
        x2, row(attn_norm_g), w_in.astype(BF16), row(b_in),
        [w_glu, w_branch_attn, w_branch_ssm, w_out, w_up, w_down], tm=256)

    attn = _attention(q.reshape(bsz, seq, ATTN_WIDTH), kv.reshape(bsz, seq, 2 * KV_WIDTH),
                      _attn_bias_table(attn_sinks))

    wb, wc, a, apow = _ssm_params(ssm_a_re, ssm_a_im, ssm_log_dt, ssm_b_re, ssm_b_im,
                                  ssm_c_re, ssm_c_im)
    perm = _scan_permutation()
    ssm = _ssm(u.reshape(bsz, seq, SSM_WIDTH), jnp.asarray(perm, BF16), jnp.asarray(perm.T, BF16),
               wb, wc, a, apow, row(ssm_d), w_glu, row(b_glu))

    x1, h2 = _merge(attn.reshape(t, ATTN_WIDTH), ssm.reshape(t, SSM_WIDTH), ga, gs, x2,
                    w_branch_attn, w_branch_ssm, w_out, row(ffn_norm_g), tm=256)

    out = _ffn(h2, x1, w_up, conv_w, row(conv_b), w_down, row(out_norm_g),
               seq=seq, tm=1024, tf=512)
    return out.reshape(bsz, seq, D_MODEL)


def kernel(x, attn_norm_g, w_in, b_in, attn_sinks, ssm_a_re, ssm_a_im, ssm_log_dt, ssm_b_re,
           ssm_b_im, ssm_c_re, ssm_c_im, ssm_d, w_glu, b_glu, w_branch_attn, w_branch_ssm, w_out,
           ffn_norm_g, w_up, conv_w, conv_b, w_down, final_norm_g):
    depth = w_in.shape[0]
    assert depth == 1, "the fused final norm assumes a single layer"
    return _layer(x, attn_norm_g[0], w_in[0], b_in[0], attn_sinks[0], ssm_a_re[0], ssm_a_im[0],
                  ssm_log_dt[0], ssm_b_re[0], ssm_b_im[0], ssm_c_re[0], ssm_c_im[0], ssm_d[0],
                  w_glu[0], b_glu[0], w_branch_attn[0], w_branch_ssm[0], w_out[0], ffn_norm_g[0],
                  w_up[0], conv_w[0], conv_b[0], w_down[0], final_norm_g)
```

```python
import functools
import math

import jax
import jax.numpy as jnp
import numpy as np
from jax import lax
from jax.experimental import pallas as pl
from jax.experimental.pallas import tpu as pltpu

D_MODEL = 2048
N_Q_HEADS = 16
N_KV_HEADS = 2
HEAD_DIM = 64
Q_PER_KV = N_Q_HEADS // N_KV_HEADS
WINDOW = 128
BLOCK = 128
ATTN_WIDTH = N_Q_HEADS * HEAD_DIM
KV_WIDTH = N_KV_HEADS * HEAD_DIM
SSM_GROUP = 16
SSM_GROUPS = 32
SSM_WIDTH = SSM_GROUP * SSM_GROUPS
SSM_STATE = 64
D_FF = 5632
CONV_WIDTH = 3
RMS_EPS = 1e-6
IN_COLS = ATTN_WIDTH + 2 * KV_WIDTH + SSM_WIDTH + 2 * D_MODEL
NEG_BIG = -1e30

V7X_LANES = 128
V7X_SUBLANES = 8
V7X_VMEM_BYTES = 64 * 1024 * 1024

BF16 = jnp.bfloat16
F32 = jnp.float32

_Q0 = 0
_K0 = ATTN_WIDTH
_V0 = _K0 + KV_WIDTH
_U0 = _V0 + KV_WIDTH
_GA0 = _U0 + SSM_WIDTH
_GS0 = _GA0 + D_MODEL

SCAN_CHUNKS = V7X_SUBLANES
SCAN_LEN = 64
SCAN_ROWS = SCAN_CHUNKS * SCAN_LEN
SSM_LANE_GROUPS = 4
LG_STATES = SSM_GROUPS * SSM_STATE // SSM_LANE_GROUPS
LG_CHANNELS = SSM_WIDTH // SSM_LANE_GROUPS


def _dot(a, b):
    return jnp.dot(a, b, preferred_element_type=F32)


def _dot_nt(a, b):
    return lax.dot_general(a, b, (((1,), (1,)), ((), ())), preferred_element_type=F32)


def _rms_norm(x, g):
    return x * lax.rsqrt(jnp.mean(x * x, axis=-1, keepdims=True) + RMS_EPS) * g


def _gelu(x):
    return 0.5 * x * (1.0 + lax.erf(x * np.float32(math.sqrt(0.5))))


def _const_spec(shape):
    nd = len(shape)
    return pl.BlockSpec(shape, lambda *_: (0,) * nd, pipeline_mode=pl.Buffered(1))


N_PROJ_OUT = 5


def _in_proj_kernel(x_ref, g_ref, w_ref, b_ref, *refs):
    n_cast = (len(refs) - N_PROJ_OUT) // 2
    cast_in = refs[:n_cast]
    q_ref, kv_ref, u_ref, ga_ref, gs_ref = refs[n_cast:n_cast + N_PROJ_OUT]
    cast_out = refs[n_cast + N_PROJ_OUT:]
    h = _rms_norm(x_ref[...], g_ref[...]).astype(BF16)
    proj = _dot(h, w_ref[...]) + b_ref[...]
    q_ref[...] = (proj[:, _Q0:_K0] * (HEAD_DIM ** -0.5)).astype(BF16)
    kv_ref[...] = proj[:, _K0:_U0].astype(BF16)
    u_ref[...] = proj[:, _U0:_GA0]
    ga_ref[...] = proj[:, _GA0:_GS0]
    gs_ref[...] = proj[:, _GS0:]
    for src, dst in zip(cast_in, cast_out):
        dst[...] = src[...].astype(BF16)


def _cast_tiling(n_rows, n_steps):
    bf16_rows = 2 * V7X_SUBLANES
    for rows in range(bf16_rows, n_rows + 1, bf16_rows):
        if n_rows % rows == 0 and n_rows // rows <= n_steps and n_steps % (n_rows // rows) == 0:
            return rows, n_steps // (n_rows // rows)
    raise ValueError(f"no cast tiling for {n_rows} rows over {n_steps} steps")


def _in_proj(x2, g, w, b, cast_weights, layer, tm):
    t = x2.shape[0]
    n_steps = t // tm
    row = lambda width: pl.BlockSpec((tm, width), lambda i: (i, 0))
    cast_in_specs, cast_out_specs, cast_shapes = [], [], []
    for cw in cast_weights:
        _, n_rows, n_cols = cw.shape
        rows, hold = _cast_tiling(n_rows, n_steps)
        cast_in_specs.append(pl.BlockSpec((None, rows, n_cols),
                                          lambda i, hold=hold: (layer, i // hold, 0)))
        cast_out_specs.append(pl.BlockSpec((rows, n_cols), lambda i, hold=hold: (i // hold, 0)))
        cast_shapes.append(jax.ShapeDtypeStruct((n_rows, n_cols), BF16))
    outs = pl.pallas_call(
        _in_proj_kernel,
        grid=(n_steps,),
        in_specs=[row(D_MODEL), _const_spec((1, D_MODEL)),
                  _const_spec((D_MODEL, IN_COLS)), _const_spec((1, IN_COLS))] + cast_in_specs,
        out_specs=[row(ATTN_WIDTH), row(2 * KV_WIDTH), row(SSM_WIDTH),
                   row(D_MODEL), row(D_MODEL)] + cast_out_specs,
        out_shape=[jax.ShapeDtypeStruct((t, ATTN_WIDTH), BF16),
                   jax.ShapeDtypeStruct((t, 2 * KV_WIDTH), BF16),
                   jax.ShapeDtypeStruct((t, SSM_WIDTH), F32),
                   jax.ShapeDtypeStruct((t, D_MODEL), F32),
                   jax.ShapeDtypeStruct((t, D_MODEL), F32)] + cast_shapes,
        compiler_params=pltpu.CompilerParams(
            dimension_semantics=("arbitrary",), vmem_limit_bytes=58 * 1024 * 1024),
        name="in_proj",
    )(x2, g, w, b, *cast_weights)
    return outs[:N_PROJ_OUT], outs[N_PROJ_OUT:]


ATTN_TQ = 1024
ATTN_SUB = ATTN_TQ // BLOCK


def _attn_bias_table(sinks):
    q_idx = np.arange(BLOCK)[:, None]
    s_idx = np.arange(2 * BLOCK)[None, :]
    dist = q_idx + BLOCK - s_idx
    band = (dist >= 0) & (dist < WINDOW)
    assert not band[:, 0].any()
    slopes = 2.0 ** (-8.0 * np.arange(1, N_Q_HEADS + 1, dtype=np.float32) / N_Q_HEADS)
    alibi = -slopes[:, None, None].astype(np.float32) * dist.astype(np.float32)[None]
    general = np.where(band[None], alibi, np.float32(NEG_BIG))
    first = np.where((band & (s_idx >= BLOCK))[None], alibi, np.float32(NEG_BIG))
    table = jnp.asarray(np.stack([general, first]).astype(np.float32))
    sink_col = jnp.asarray((s_idx == 0)[None, None])
    return jnp.where(sink_col, sinks.astype(F32)[None, :, None, None], table)


ATTN_PATCH = 16


def _attn_kernel(q_ref, kvc_ref, kvp_ref, bias_ref, o_ref, ops_scr):
    n = pl.program_id(1)
    lane = lax.broadcasted_iota(jnp.int32, (1, 2 * HEAD_DIM), 1)
    lo = lane < HEAD_DIM

    def operands(kv):
        kv = kv.astype(F32)
        k_nat = kv[:, :KV_WIDTH]
        v_nat = kv[:, KV_WIDTH:]
        k_swp = pltpu.roll(k_nat, HEAD_DIM, 1)
        v_swp = pltpu.roll(v_nat, HEAD_DIM, 1)
        return (jnp.where(lo, k_nat, 0.0), jnp.where(lo, k_swp, 0.0),
                jnp.where(lo, 0.0, k_swp), jnp.where(lo, 0.0, k_nat),
                jnp.where(lo, v_nat, 1.0), jnp.where(lo, v_swp, 1.0),
                jnp.where(lo, 1.0, v_swp), jnp.where(lo, 1.0, v_nat))

    for idx, op in enumerate(operands(kvp_ref[...])):
        ops_scr[idx, 0:BLOCK, :] = op.astype(BF16)
    for idx, op in enumerate(operands(kvc_ref[...])):
        ops_scr[idx, BLOCK:, :] = op.astype(BF16)

    sink_row = lax.broadcasted_iota(jnp.int32, (ATTN_PATCH, 1), 0) == 0
    zero = jnp.zeros((1, 2 * HEAD_DIM), F32)
    sink_fill = (zero, zero, zero, zero,
                 jnp.where(lo, 0.0, 1.0), jnp.where(lo, 0.0, 1.0),
                 jnp.where(lo, 1.0, 0.0), jnp.where(lo, 1.0, 0.0))

    def window(idx, r0):
        top = ops_scr[idx, pl.ds(r0, ATTN_PATCH), :].astype(F32)
        top = jnp.where(sink_row, sink_fill[idx], top).astype(BF16)
        rest = ops_scr[idx, pl.ds(r0 + ATTN_PATCH, 2 * BLOCK - ATTN_PATCH), :]
        return jnp.concatenate([top, rest], axis=0)

    def sub_block(sb, carry):
        r0 = pl.multiple_of(sb * BLOCK, BLOCK)
        first = jnp.where(jnp.logical_and(n == 0, sb == 0), 1, 0)
        win = [window(idx, r0) for idx in range(8)]
        q_all = q_ref[pl.ds(r0, BLOCK), :]
        for pair in range(N_Q_HEADS // 2):
            grp = pair // (Q_PER_KV // 2)
            qp = q_all[:, pair * 2 * HEAD_DIM:(pair + 1) * 2 * HEAD_DIM]
            res = []
            for odd in range(2):
                head = 2 * pair + odd
                kk = win[2 * odd + grp]
                vv = win[4 + 2 * odd + grp]
                s = _dot_nt(qp, kk) + bias_ref[first, head]
                m = jnp.max(s, axis=-1, keepdims=True)
                p = jnp.exp(s - m).astype(BF16)
                res.append(_dot(p, vv))
            num = jnp.where(lo, res[0], res[1])
            den = pltpu.roll(jnp.where(lo, res[1], res[0]), HEAD_DIM, 1)
            o_ref[pl.ds(r0, BLOCK), pair * 2 * HEAD_DIM:(pair + 1) * 2 * HEAD_DIM] = (
                num / den).astype(BF16)
        return carry

    lax.fori_loop(0, ATTN_SUB, sub_block, 0)


def _attention(q, kv, bias):
    bsz, seq, _ = q.shape
    nq = seq // ATTN_TQ
    return pl.pallas_call(
        _attn_kernel,
        grid=(bsz, nq),
        in_specs=[
            pl.BlockSpec((None, ATTN_TQ, ATTN_WIDTH), lambda b, n: (b, n, 0)),
            pl.BlockSpec((None, ATTN_TQ, 2 * KV_WIDTH), lambda b, n: (b, n, 0)),
            pl.BlockSpec((None, BLOCK, 2 * KV_WIDTH),
                         lambda b, n: (b, jnp.maximum(n * ATTN_SUB - 1, 0), 0)),
            _const_spec((2, N_Q_HEADS, BLOCK, 2 * BLOCK)),
        ],
        out_specs=pl.BlockSpec((None, ATTN_TQ, ATTN_WIDTH), lambda b, n: (b, n, 0)),
        out_shape=jax.ShapeDtypeStruct((bsz, seq, ATTN_WIDTH), BF16),
        scratch_shapes=[pltpu.VMEM((8, ATTN_TQ + BLOCK, 2 * HEAD_DIM), BF16)],
        compiler_params=pltpu.CompilerParams(
            dimension_semantics=("arbitrary", "arbitrary"), vmem_limit_bytes=32 * 1024 * 1024),
        name="attn",
    )(q, kv, kv, bias)


def _scan_permutation():
    p = np.zeros((SCAN_ROWS, SCAN_ROWS), np.float32)
    for c in range(SCAN_CHUNKS):
        for s in range(SCAN_LEN):
            p[s * SCAN_CHUNKS + c, c * SCAN_LEN + s] = 1.0
    return p


def _ssm_kernel(u_ref, perm_ref, permt_ref, wb_ref, wc_ref, a_ref, apow_ref, d_ref,
                wglu_ref, bglu_ref, o_ref, carry_scr, bu_scr, xs_scr, x0_scr, y_scr):
    @pl.when(pl.program_id(1) == 0)
    def _():
        carry_scr[...] = jnp.zeros_like(carry_scr)

    u = u_ref[...]
    u_hi = u.astype(BF16)
    u_lo = (u - u_hi.astype(F32)).astype(BF16)
    perm = perm_ref[...]
    up_hi = _dot(perm, u_hi)
    u_perm = up_hi + _dot(perm, u_lo)
    ub = up_hi.astype(BF16)

    def recurrence(lg, a_re, a_im, state, store):
        xr, xi = state
        for s in range(SCAN_LEN):
            rows = slice(s * SCAN_CHUNKS, (s + 1) * SCAN_CHUNKS)
            br = bu_scr[lg, rows, :LG_STATES]
            bi = bu_scr[lg, rows, LG_STATES:]
            xr, xi = a_re * xr - a_im * xi + br, a_re * xi + a_im * xr + bi
            if store:
                xs_scr[lg, rows, :LG_STATES] = xr
                xs_scr[lg, rows, LG_STATES:] = xi
        return xr, xi

    for lg in range(SSM_LANE_GROUPS):
        bu_scr[lg] = _dot(ub[:, lg * LG_CHANNELS:(lg + 1) * LG_CHANNELS], wb_ref[lg])
        a_re = jnp.broadcast_to(a_ref[lg, 0:1, :], (SCAN_CHUNKS, LG_STATES))
        a_im = jnp.broadcast_to(a_ref[lg, 1:2, :], (SCAN_CHUNKS, LG_STATES))
        zeros = jnp.zeros((SCAN_CHUNKS, LG_STATES), F32)
        e_re, e_im = recurrence(lg, a_re, a_im, (zeros, zeros), False)
        p_re = apow_ref[lg, 0:1, :]
        p_im = apow_ref[lg, 1:2, :]
        cur_re = carry_scr[lg, 0:1, :]
        cur_im = carry_scr[lg, 1:2, :]
        for c in range(SCAN_CHUNKS):
            x0_scr[lg, 0, c:c + 1, :] = cur_re
            x0_scr[lg, 1, c:c + 1, :] = cur_im
            nxt_re = p_re * cur_re - p_im * cur_im + e_re[c:c + 1, :]
            nxt_im = p_re * cur_im + p_im * cur_re + e_im[c:c + 1, :]
            cur_re, cur_im = nxt_re, nxt_im
        carry_scr[lg, 0:1, :] = cur_re
        carry_scr[lg, 1:2, :] = cur_im
        recurrence(lg, a_re, a_im, (x0_scr[lg, 0], x0_scr[lg, 1]), True)
        y_scr[:, lg * LG_CHANNELS:(lg + 1) * LG_CHANNELS] = _dot(
            xs_scr[lg].astype(BF16), wc_ref[lg])

    y = y_scr[...] + d_ref[...] * u_perm
    z = _dot(_gelu(y).astype(BF16), wglu_ref[...]) + bglu_ref[...]
    ssm = z[:, :SSM_WIDTH] * jax.nn.sigmoid(z[:, SSM_WIDTH:])
    o_ref[...] = _dot(permt_ref[...], ssm.astype(BF16)).astype(BF16)


def _ssm(u, perm, permt, wb, wc, a, apow, d, wglu, bglu):
    bsz, seq, _ = u.shape
    return pl.pallas_call(
        _ssm_kernel,
        grid=(bsz, seq // SCAN_ROWS),
        in_specs=[
            pl.BlockSpec((None, SCAN_ROWS, SSM_WIDTH), lambda b, s: (b, s, 0)),
            _const_spec((SCAN_ROWS, SCAN_ROWS)),
            _const_spec((SCAN_ROWS, SCAN_ROWS)),
            _const_spec((SSM_LANE_GROUPS, LG_CHANNELS, 2 * LG_STATES)),
            _const_spec((SSM_LANE_GROUPS, 2 * LG_STATES, LG_CHANNELS)),
            _const_spec((SSM_LANE_GROUPS, 2, LG_STATES)),
            _const_spec((SSM_LANE_GROUPS, 2, LG_STATES)),
            _const_spec((1, SSM_WIDTH)),
            _const_spec((SSM_WIDTH, 2 * SSM_WIDTH)),
            _const_spec((1, 2 * SSM_WIDTH)),
        ],
        out_specs=pl.BlockSpec((None, SCAN_ROWS, SSM_WIDTH), lambda b, s: (b, s, 0)),
        out_shape=jax.ShapeDtypeStruct((bsz, seq, SSM_WIDTH), BF16),
        scratch_shapes=[
            pltpu.VMEM((SSM_LANE_GROUPS, 2, LG_STATES), F32),
            pltpu.VMEM((SSM_LANE_GROUPS, SCAN_ROWS, 2 * LG_STATES), F32),
            pltpu.VMEM((SSM_LANE_GROUPS, SCAN_ROWS, 2 * LG_STATES), F32),
            pltpu.VMEM((SSM_LANE_GROUPS, 2, SCAN_CHUNKS, LG_STATES), F32),
            pltpu.VMEM((SCAN_ROWS, SSM_WIDTH), F32),
        ],
        compiler_params=pltpu.CompilerParams(
            dimension_semantics=("arbitrary", "arbitrary"), vmem_limit_bytes=40 * 1024 * 1024),
        name="ssm",
    )(u, perm, permt, wb, wc, a, apow, d, wglu, bglu)


def _ssm_params(a_re, a_im, log_dt, b_re, b_im, c_re, c_im):
    dt = jnp.exp(log_dt)[:, None]
    mag = jnp.exp(a_re * dt)
    ab_re = mag * jnp.cos(a_im * dt)
    ab_im = mag * jnp.sin(a_im * dt)
    nr = ab_re - 1.0
    ni = ab_im
    den = a_re * a_re + a_im * a_im
    z_re = (nr * a_re + ni * a_im) / den
    z_im = (ni * a_re - nr * a_im) / den
    bb_re = z_re[..., None] * b_re - z_im[..., None] * b_im
    bb_im = z_re[..., None] * b_im + z_im[..., None] * b_re

    gl = SSM_GROUPS // SSM_LANE_GROUPS
    eye = jnp.eye(gl, dtype=F32)

    def in_mat(bb):
        bb = bb.reshape(SSM_LANE_GROUPS, gl, SSM_STATE, SSM_GROUP)
        m = jnp.einsum('lgph,gk->lghkp', bb, eye)
        return m.reshape(SSM_LANE_GROUPS, LG_CHANNELS, LG_STATES)

    def out_mat(cc):
        cc = cc.reshape(SSM_LANE_GROUPS, gl, SSM_GROUP, SSM_STATE)
        m = jnp.einsum('lghp,gk->lgpkh', cc, eye)
        return m.reshape(SSM_LANE_GROUPS, LG_STATES, LG_CHANNELS)

    wb = jnp.concatenate([in_mat(bb_re), in_mat(bb_im)], axis=-1).astype(BF16)
    wc = jnp.concatenate([out_mat(c_re), out_mat(-c_im)], axis=-2).astype(BF16)

    pr, pi = ab_re, ab_im
    for _ in range(int(math.log2(SCAN_LEN))):
        pr, pi = pr * pr - pi * pi, 2.0 * pr * pi
    lanes = lambda v: v.reshape(SSM_LANE_GROUPS, LG_STATES)
    a = jnp.stack([lanes(ab_re), lanes(ab_im)], axis=1)
    apow = jnp.stack([lanes(pr), lanes(pi)], axis=1)
    return wb, wc, a, apow


def _merge_kernel(attn_ref, ssm_ref, ga_ref, gs_ref, x_ref, wba_ref, wbs_ref, wout_ref,
                  g_ref, x1_ref, h2_ref):
    a = _dot(attn_ref[...], wba_ref[...])
    s = _dot(ssm_ref[...], wbs_ref[...])
    merged = jax.nn.sigmoid(ga_ref[...]) * a + jax.nn.sigmoid(gs_ref[...]) * s
    x1 = x_ref[...] + _dot(merged.astype(BF16), wout_ref[...])
    x1_ref[...] = x1
    h2_ref[...] = _rms_norm(x1, g_ref[...]).astype(BF16)


def _merge(attn, ssm, ga, gs, x2, wba, wbs, wout, g, tm):
    t = x2.shape[0]
    row = lambda width: pl.BlockSpec((tm, width), lambda i: (i, 0))
    return pl.pallas_call(
        _merge_kernel,
        grid=(t // tm,),
        in_specs=[row(ATTN_WIDTH), row(SSM_WIDTH), row(D_MODEL), row(D_MODEL), row(D_MODEL),
                  _const_spec((ATTN_WIDTH, D_MODEL)), _const_spec((SSM_WIDTH, D_MODEL)),
                  _const_spec((D_MODEL, D_MODEL)), _const_spec((1, D_MODEL))],
        out_specs=[row(D_MODEL), row(D_MODEL)],
        out_shape=[jax.ShapeDtypeStruct((t, D_MODEL), F32),
                   jax.ShapeDtypeStruct((t, D_MODEL), BF16)],
        compiler_params=pltpu.CompilerParams(
            dimension_semantics=("arbitrary",), vmem_limit_bytes=56 * 1024 * 1024),
        name="merge",
    )(attn, ssm, ga, gs, x2, wba, wbs, wout, g)


FFN_TAIL = V7X_SUBLANES
FFN_CHUNK = 256


def _ffn_kernel(h_ref, x1_hbm, wv_ref, wg_ref, cw_ref, cb_ref, wd_ref, gf_ref,
                o_ref, x1_scr, act_scr, tail_scr, x1_sem, *, tm, tf, seq):
    i = pl.program_id(0)
    j = pl.program_id(1)
    last = pl.num_programs(1) - 1

    def x1_copy():
        return pltpu.make_async_copy(x1_hbm.at[pl.ds(i * tm, tm), :], x1_scr, x1_sem)

    @pl.when(jnp.logical_and(i == 0, j == 0))
    def _():
        tail_scr[...] = jnp.zeros_like(tail_scr)

    @pl.when(j == 0)
    def _():
        x1_copy().start()
        o_ref[...] = jnp.zeros_like(o_ref)

    seq_start = (i * tm) % seq == 0
    row = lax.broadcasted_iota(jnp.int32, (FFN_TAIL, FFN_CHUNK), 0)
    h = h_ref[...]
    for c in range(tf // FFN_CHUNK):
        cols = slice(c * FFN_CHUNK, (c + 1) * FFN_CHUNK)
        val = _dot(h, wv_ref[:, cols])
        gate = _dot(h, wg_ref[:, cols])
        prev = jnp.where(seq_start, 0.0, tail_scr[j, :, cols])
        tail_scr[j, :, cols] = gate[tm - FFN_TAIL:, :]
        conv = cb_ref[:, cols] + cw_ref[CONV_WIDTH - 1:CONV_WIDTH, cols] * gate
        for back in range(1, CONV_WIDTH):
            shifted = pltpu.roll(gate, back, 0)
            top = jnp.where(row < back, pltpu.roll(prev, back, 0), shifted[:FFN_TAIL])
            shifted = jnp.concatenate([top, shifted[FFN_TAIL:]], axis=0)
            k = CONV_WIDTH - 1 - back
            conv = conv + cw_ref[k:k + 1, cols] * shifted
        act_scr[:, cols] = (val * _gelu(conv)).astype(BF16)
    o_ref[...] += _dot(act_scr[...], wd_ref[...])

    @pl.when(j == last)
    def _():
        x1_copy().wait()
        o_ref[...] = _rms_norm(x1_scr[...] + o_ref[...], gf_ref[...])


def _ffn(h2, x1, w_up, conv_w, conv_b, w_down, gf, seq, tm, tf):
    t = h2.shape[0]
    nf = D_FF // tf
    kern = functools.partial(_ffn_kernel, tm=tm, tf=tf, seq=seq)
    return pl.pallas_call(
        kern,
        grid=(t // tm, nf),
        in_specs=[
            pl.BlockSpec((tm, D_MODEL), lambda i, j: (i, 0)),
            pl.BlockSpec(memory_space=pl.ANY),
            pl.BlockSpec((D_MODEL, tf), lambda i, j: (0, j)),
            pl.BlockSpec((D_MODEL, tf), lambda i, j: (0, j + nf)),
            pl.BlockSpec((CONV_WIDTH, tf), lambda i, j: (0, j)),
            pl.BlockSpec((1, tf), lambda i, j: (0, j)),
            pl.BlockSpec((tf, D_MODEL), lambda i, j: (j, 0)),
            _const_spec((1, D_MODEL)),
        ],
        out_specs=pl.BlockSpec((tm, D_MODEL), lambda i, j: (i, 0)),
        out_shape=jax.ShapeDtypeStruct((t, D_MODEL), F32),
        scratch_shapes=[pltpu.VMEM((tm, D_MODEL), F32),
                        pltpu.VMEM((tm, tf), BF16),
                        pltpu.VMEM((nf, FFN_TAIL, tf), F32),
                        pltpu.SemaphoreType.DMA(())],
        compiler_params=pltpu.CompilerParams(
            dimension_semantics=("arbitrary", "arbitrary"), vmem_limit_bytes=60 * 1024 * 1024),
        name="ffn",
    )(h2, x1, w_up, w_up, conv_w, conv_b, w_down, gf)


def _layer(layer, x, attn_norm_g, w_in, b_in, attn_sinks, ssm_a_re, ssm_a_im, ssm_log_dt,
           ssm_b_re, ssm_b_im, ssm_c_re, ssm_c_im, ssm_d, b_glu, ffn_norm_g, conv_w, conv_b,
           out_norm_g, stacked_weights):
    bsz, seq, _ = x.shape
    t = bsz * seq
    x2 = x.reshape(t, D_MODEL)
    row = lambda v: v.reshape(1, -1)

    (q, kv, u, ga, gs), (w_glu, w_branch_attn, w_branch_ssm, w_out, w_up, w_down) = _in_proj(
        x2, row(attn_norm_g), w_in.astype(BF16), row(b_in), stacked_weights, layer, tm=256)

    attn = _attention(q.reshape(bsz, seq, ATTN_WIDTH), kv.reshape(bsz, seq, 2 * KV_WIDTH),
                      _attn_bias_table(attn_sinks))

    wb, wc, a, apow = _ssm_params(ssm_a_re, ssm_a_im, ssm_log_dt, ssm_b_re, ssm_b_im,
                                  ssm_c_re, ssm_c_im)
    perm = _scan_permutation()
    ssm = _ssm(u.reshape(bsz, seq, SSM_WIDTH), jnp.asarray(perm, BF16), jnp.asarray(perm.T, BF16),
               wb, wc, a, apow, row(ssm_d), w_glu, row(b_glu))

    x1, h2 = _merge(attn.reshape(t, ATTN_WIDTH), ssm.reshape(t, SSM_WIDTH), ga, gs, x2,
                    w_branch_attn, w_branch_ssm, w_out, row(ffn_norm_g), tm=256)

    out = _ffn(h2, x1, w_up, conv_w, row(conv_b), w_down, row(out_norm_g),
               seq=seq, tm=1024, tf=512)
    return out.reshape(bsz, seq, D_MODEL)


def kernel(x, attn_norm_g, w_in, b_in, attn_sinks, ssm_a_re, ssm_a_im, ssm_log_dt, ssm_b_re,
           ssm_b_im, ssm_c_re, ssm_c_im, ssm_d, w_glu, b_glu, w_branch_attn, w_branch_ssm, w_out,
           ffn_norm_g, w_up, conv_w, conv_b, w_down, final_norm_g):
    depth = w_in.shape[0]
    assert depth == 1, "the fused final norm assumes a single layer"
    return _layer(0, x, attn_norm_g[0], w_in[0], b_in[0], attn_sinks[0], ssm_a_re[0], ssm_a_im[0],
                  ssm_log_dt[0], ssm_b_re[0], ssm_b_im[0], ssm_c_re[0], ssm_c_im[0], ssm_d[0],
                  b_glu[0], ffn_norm_g[0], conv_w[0], conv_b[0], final_norm_g,
                  [w_glu, w_branch_attn, w_branch_ssm, w_out, w_up, w_down])
```

```python
import functools
import math

import jax
import jax.numpy as jnp
import numpy as np
from jax import lax
from jax.experimental import pallas as pl
from jax.experimental.pallas import tpu as pltpu

D_MODEL = 2048
N_Q_HEADS = 16
N_KV_HEADS = 2
HEAD_DIM = 64
Q_PER_KV = N_Q_HEADS // N_KV_HEADS
WINDOW = 128
BLOCK = 128
ATTN_WIDTH = N_Q_HEADS * HEAD_DIM
KV_WIDTH = N_KV_HEADS * HEAD_DIM
SSM_GROUP = 16
SSM_GROUPS = 32
SSM_WIDTH = SSM_GROUP * SSM_GROUPS
SSM_STATE = 64
D_FF = 5632
CONV_WIDTH = 3
RMS_EPS = 1e-6
IN_COLS = ATTN_WIDTH + 2 * KV_WIDTH + SSM_WIDTH + 2 * D_MODEL
NEG_BIG = -1e30

V7X_LANES = 128
V7X_SUBLANES = 8
V7X_VMEM_BYTES = 64 * 1024 * 1024

BF16 = jnp.bfloat16
F32 = jnp.float32

_Q0 = 0
_K0 = ATTN_WIDTH
_V0 = _K0 + KV_WIDTH
_U0 = _V0 + KV_WIDTH
_GA0 = _U0 + SSM_WIDTH
_GS0 = _GA0 + D_MODEL

SCAN_CHUNKS = V7X_SUBLANES
SCAN_LEN = 64
SCAN_ROWS = SCAN_CHUNKS * SCAN_LEN
SSM_LANE_GROUPS = 4
LG_STATES = SSM_GROUPS * SSM_STATE // SSM_LANE_GROUPS
LG_CHANNELS = SSM_WIDTH // SSM_LANE_GROUPS


def _dot(a, b):
    return jnp.dot(a, b, preferred_element_type=F32)


def _dot_nt(a, b):
    return lax.dot_general(a, b, (((1,), (1,)), ((), ())), preferred_element_type=F32)


def _rms_norm(x, g):
    return x * lax.rsqrt(jnp.mean(x * x, axis=-1, keepdims=True) + RMS_EPS) * g


def _gelu(x):
    return 0.5 * x * (1.0 + lax.erf(x * np.float32(math.sqrt(0.5))))


def _const_spec(shape):
    nd = len(shape)
    return pl.BlockSpec(shape, lambda *_: (0,) * nd, pipeline_mode=pl.Buffered(1))


N_PROJ_OUT = 5


def _in_proj_kernel(x_ref, g_ref, w_ref, b_ref, *refs):
    n_cast = (len(refs) - N_PROJ_OUT) // 2
    cast_in = refs[:n_cast]
    q_ref, kv_ref, u_ref, ga_ref, gs_ref = refs[n_cast:n_cast + N_PROJ_OUT]
    cast_out = refs[n_cast + N_PROJ_OUT:]
    h = _rms_norm(x_ref[...], g_ref[...]).astype(BF16)
    proj = _dot(h, w_ref[...]) + b_ref[...]
    q_ref[...] = (proj[:, _Q0:_K0] * (HEAD_DIM ** -0.5)).astype(BF16)
    kv_ref[...] = proj[:, _K0:_U0].astype(BF16)
    u_ref[...] = proj[:, _U0:_GA0]
    ga_ref[...] = proj[:, _GA0:_GS0]
    gs_ref[...] = proj[:, _GS0:]
    for src, dst in zip(cast_in, cast_out):
        dst[...] = src[...].astype(BF16)


def _cast_tiling(n_rows, n_steps):
    bf16_rows = 2 * V7X_SUBLANES
    for rows in range(bf16_rows, n_rows + 1, bf16_rows):
        if n_rows % rows == 0 and n_rows // rows <= n_steps and n_steps % (n_rows // rows) == 0:
            return rows, n_steps // (n_rows // rows)
    raise ValueError(f"no cast tiling for {n_rows} rows over {n_steps} steps")


def _in_proj(x2, g, w, b, cast_weights, layer, tm):
    t = x2.shape[0]
    n_steps = t // tm
    row = lambda width: pl.BlockSpec((tm, width), lambda i: (i, 0))
    cast_in_specs, cast_out_specs, cast_shapes = [], [], []
    for cw in cast_weights:
        _, n_rows, n_cols = cw.shape
        rows, hold = _cast_tiling(n_rows, n_steps)
        cast_in_specs.append(pl.BlockSpec((None, rows, n_cols),
                                          lambda i, hold=hold: (layer, i // hold, 0)))
        cast_out_specs.append(pl.BlockSpec((rows, n_cols), lambda i, hold=hold: (i // hold, 0)))
        cast_shapes.append(jax.ShapeDtypeStruct((n_rows, n_cols), BF16))
    outs = pl.pallas_call(
        _in_proj_kernel,
        grid=(n_steps,),
        in_specs=[row(D_MODEL), _const_spec((1, D_MODEL)),
                  _const_spec((D_MODEL, IN_COLS)), _const_spec((1, IN_COLS))] + cast_in_specs,
        out_specs=[row(ATTN_WIDTH), row(2 * KV_WIDTH), row(SSM_WIDTH),
                   row(D_MODEL), row(D_MODEL)] + cast_out_specs,
        out_shape=[jax.ShapeDtypeStruct((t, ATTN_WIDTH), BF16),
                   jax.ShapeDtypeStruct((t, 2 * KV_WIDTH), BF16),
                   jax.ShapeDtypeStruct((t, SSM_WIDTH), F32),
                   jax.ShapeDtypeStruct((t, D_MODEL), F32),
                   jax.ShapeDtypeStruct((t, D_MODEL), F32)] + cast_shapes,
        compiler_params=pltpu.CompilerParams(
            dimension_semantics=("arbitrary",), vmem_limit_bytes=58 * 1024 * 1024),
        name="in_proj",
    )(x2, g, w, b, *cast_weights)
    return outs[:N_PROJ_OUT], outs[N_PROJ_OUT:]


ATTN_TQ = 1024
ATTN_SUB = ATTN_TQ // BLOCK


def _attn_bias_table(sinks):
    q_idx = np.arange(BLOCK)[:, None]
    s_idx = np.arange(2 * BLOCK)[None, :]
    dist = q_idx + BLOCK - s_idx
    band = (dist >= 0) & (dist < WINDOW)
    assert not band[:, 0].any()
    slopes = 2.0 ** (-8.0 * np.arange(1, N_Q_HEADS + 1, dtype=np.float32) / N_Q_HEADS)
    alibi = -slopes[:, None, None].astype(np.float32) * dist.astype(np.float32)[None]
    general = np.where(band[None], alibi, np.float32(NEG_BIG))
    first = np.where((band & (s_idx >= BLOCK))[None], alibi, np.float32(NEG_BIG))
    table = jnp.asarray(np.stack([general, first]).astype(np.float32))
    sink_col = jnp.asarray((s_idx == 0)[None, None])
    return jnp.where(sink_col, sinks.astype(F32)[None, :, None, None], table)


ATTN_PATCH = 16


def _attn_kernel(q_ref, kvc_ref, kvp_ref, bias_ref, o_ref, ops_scr):
    n = pl.program_id(1)
    lane = lax.broadcasted_iota(jnp.int32, (1, 2 * HEAD_DIM), 1)
    lo = lane < HEAD_DIM

    def operands(kv):
        kv = kv.astype(F32)
        k_nat = kv[:, :KV_WIDTH]
        v_nat = kv[:, KV_WIDTH:]
        k_swp = pltpu.roll(k_nat, HEAD_DIM, 1)
        v_swp = pltpu.roll(v_nat, HEAD_DIM, 1)
        return (jnp.where(lo, k_nat, 0.0), jnp.where(lo, k_swp, 0.0),
                jnp.where(lo, 0.0, k_swp), jnp.where(lo, 0.0, k_nat),
                jnp.where(lo, v_nat, 1.0), jnp.where(lo, v_swp, 1.0),
                jnp.where(lo, 1.0, v_swp), jnp.where(lo, 1.0, v_nat))

    for idx, op in enumerate(operands(kvp_ref[...])):
        ops_scr[idx, 0:BLOCK, :] = op.astype(BF16)
    for idx, op in enumerate(operands(kvc_ref[...])):
        ops_scr[idx, BLOCK:, :] = op.astype(BF16)

    sink_row = lax.broadcasted_iota(jnp.int32, (ATTN_PATCH, 1), 0) == 0
    zero = jnp.zeros((1, 2 * HEAD_DIM), F32)
    sink_fill = (zero, zero, zero, zero,
                 jnp.where(lo, 0.0, 1.0), jnp.where(lo, 0.0, 1.0),
                 jnp.where(lo, 1.0, 0.0), jnp.where(lo, 1.0, 0.0))

    def window(idx, r0):
        top = ops_scr[idx, pl.ds(r0, ATTN_PATCH), :].astype(F32)
        top = jnp.where(sink_row, sink_fill[idx], top).astype(BF16)
        rest = ops_scr[idx, pl.ds(r0 + ATTN_PATCH, 2 * BLOCK - ATTN_PATCH), :]
        return jnp.concatenate([top, rest], axis=0)

    def sub_block(sb, carry):
        r0 = pl.multiple_of(sb * BLOCK, BLOCK)
        first = jnp.where(jnp.logical_and(n == 0, sb == 0), 1, 0)
        win = [window(idx, r0) for idx in range(8)]
        q_all = q_ref[pl.ds(r0, BLOCK), :]
        pairs_per_group = Q_PER_KV // 2
        for grp in range(N_KV_HEADS):
            pairs = range(grp * pairs_per_group, (grp + 1) * pairs_per_group)
            q_stack = jnp.concatenate(
                [q_all[:, pr * 2 * HEAD_DIM:(pr + 1) * 2 * HEAD_DIM] for pr in pairs], axis=0)
            res = []
            for odd in range(2):
                bias = jnp.concatenate([bias_ref[first, 2 * pr + odd] for pr in pairs], axis=0)
                s = _dot_nt(q_stack, win[2 * odd + grp]) + bias
                m = jnp.max(s, axis=-1, keepdims=True)
                p = jnp.exp(s - m).astype(BF16)
                res.append(_dot(p, win[4 + 2 * odd + grp]))
            num = jnp.where(lo, res[0], res[1])
            den = pltpu.roll(jnp.where(lo, res[1], res[0]), HEAD_DIM, 1)
            out = (num / den).astype(BF16)
            for k, pr in enumerate(pairs):
                o_ref[pl.ds(r0, BLOCK), pr * 2 * HEAD_DIM:(pr + 1) * 2 * HEAD_DIM] = (
                    out[k * BLOCK:(k + 1) * BLOCK, :])
        return carry

    lax.fori_loop(0, ATTN_SUB, sub_block, 0)


def _attention(q, kv, bias):
    bsz, seq, _ = q.shape
    nq = seq // ATTN_TQ
    return pl.pallas_call(
        _attn_kernel,
        grid=(bsz, nq),
        in_specs=[
            pl.BlockSpec((None, ATTN_TQ, ATTN_WIDTH), lambda b, n: (b, n, 0)),
            pl.BlockSpec((None, ATTN_TQ, 2 * KV_WIDTH), lambda b, n: (b, n, 0)),
            pl.BlockSpec((None, BLOCK, 2 * KV_WIDTH),
                         lambda b, n: (b, jnp.maximum(n * ATTN_SUB - 1, 0), 0)),
            _const_spec((2, N_Q_HEADS, BLOCK, 2 * BLOCK)),
        ],
        out_specs=pl.BlockSpec((None, ATTN_TQ, ATTN_WIDTH), lambda b, n: (b, n, 0)),
        out_shape=jax.ShapeDtypeStruct((bsz, seq, ATTN_WIDTH), BF16),
        scratch_shapes=[pltpu.VMEM((8, ATTN_TQ + BLOCK, 2 * HEAD_DIM), BF16)],
        compiler_params=pltpu.CompilerParams(
            dimension_semantics=("arbitrary", "arbitrary"), vmem_limit_bytes=32 * 1024 * 1024),
        name="attn",
    )(q, kv, kv, bias)


def _scan_permutation():
    p = np.zeros((SCAN_ROWS, SCAN_ROWS), np.float32)
    for c in range(SCAN_CHUNKS):
        for s in range(SCAN_LEN):
            p[s * SCAN_CHUNKS + c, c * SCAN_LEN + s] = 1.0
    return p


def _ssm_kernel(u_ref, perm_ref, permt_ref, wb_ref, wc_ref, a_ref, apow_ref, d_ref,
                wglu_ref, bglu_ref, o_ref, carry_scr, bu_scr, xs_scr, x0_scr, y_scr):
    @pl.when(pl.program_id(1) == 0)
    def _():
        carry_scr[...] = jnp.zeros_like(carry_scr)

    u = u_ref[...]
    u_hi = u.astype(BF16)
    u_lo = (u - u_hi.astype(F32)).astype(BF16)
    perm = perm_ref[...]
    up_hi = _dot(perm, u_hi)
    u_perm = up_hi + _dot(perm, u_lo)
    ub = up_hi.astype(BF16)

    def recurrence(lg, a_re, a_im, state, store):
        xr, xi = state
        for s in range(SCAN_LEN):
            rows = slice(s * SCAN_CHUNKS, (s + 1) * SCAN_CHUNKS)
            br = bu_scr[lg, rows, :LG_STATES]
            bi = bu_scr[lg, rows, LG_STATES:]
            xr, xi = a_re * xr - a_im * xi + br, a_re * xi + a_im * xr + bi
            if store:
                xs_scr[lg, rows, :LG_STATES] = xr
                xs_scr[lg, rows, LG_STATES:] = xi
        return xr, xi

    for lg in range(SSM_LANE_GROUPS):
        bu_scr[lg] = _dot(ub[:, lg * LG_CHANNELS:(lg + 1) * LG_CHANNELS], wb_ref[lg])
        a_re = jnp.broadcast_to(a_ref[lg, 0:1, :], (SCAN_CHUNKS, LG_STATES))
        a_im = jnp.broadcast_to(a_ref[lg, 1:2, :], (SCAN_CHUNKS, LG_STATES))
        zeros = jnp.zeros((SCAN_CHUNKS, LG_STATES), F32)
        e_re, e_im = recurrence(lg, a_re, a_im, (zeros, zeros), False)
        p_re = apow_ref[lg, 0:1, :]
        p_im = apow_ref[lg, 1:2, :]
        cur_re = carry_scr[lg, 0:1, :]
        cur_im = carry_scr[lg, 1:2, :]
        for c in range(SCAN_CHUNKS):
            x0_scr[lg, 0, c:c + 1, :] = cur_re
            x0_scr[lg, 1, c:c + 1, :] = cur_im
            nxt_re = p_re * cur_re - p_im * cur_im + e_re[c:c + 1, :]
            nxt_im = p_re * cur_im + p_im * cur_re + e_im[c:c + 1, :]
            cur_re, cur_im = nxt_re, nxt_im
        carry_scr[lg, 0:1, :] = cur_re
        carry_scr[lg, 1:2, :] = cur_im
        recurrence(lg, a_re, a_im, (x0_scr[lg, 0], x0_scr[lg, 1]), True)
        y_scr[:, lg * LG_CHANNELS:(lg + 1) * LG_CHANNELS] = _dot(
            xs_scr[lg].astype(BF16), wc_ref[lg])

    y = y_scr[...] + d_ref[...] * u_perm
    z = _dot(_gelu(y).astype(BF16), wglu_ref[...]) + bglu_ref[...]
    ssm = z[:, :SSM_WIDTH] * jax.nn.sigmoid(z[:, SSM_WIDTH:])
    o_ref[...] = _dot(permt_ref[...], ssm.astype(BF16)).astype(BF16)


def _ssm(u, perm, permt, wb, wc, a, apow, d, wglu, bglu):
    bsz, seq, _ = u.shape
    return pl.pallas_call(
        _ssm_kernel,
        grid=(bsz, seq // SCAN_ROWS),
        in_specs=[
            pl.BlockSpec((None, SCAN_ROWS, SSM_WIDTH), lambda b, s: (b, s, 0)),
            _const_spec((SCAN_ROWS, SCAN_ROWS)),
            _const_spec((SCAN_ROWS, SCAN_ROWS)),
            _const_spec((SSM_LANE_GROUPS, LG_CHANNELS, 2 * LG_STATES)),
            _const_spec((SSM_LANE_GROUPS, 2 * LG_STATES, LG_CHANNELS)),
            _const_spec((SSM_LANE_GROUPS, 2, LG_STATES)),
            _const_spec((SSM_LANE_GROUPS, 2, LG_STATES)),
            _const_spec((1, SSM_WIDTH)),
            _const_spec((SSM_WIDTH, 2 * SSM_WIDTH)),
            _const_spec((1, 2 * SSM_WIDTH)),
        ],
        out_specs=pl.BlockSpec((None, SCAN_ROWS, SSM_WIDTH), lambda b, s: (b, s, 0)),
        out_shape=jax.ShapeDtypeStruct((bsz, seq, SSM_WIDTH), BF16),
        scratch_shapes=[
            pltpu.VMEM((SSM_LANE_GROUPS, 2, LG_STATES), F32),
            pltpu.VMEM((SSM_LANE_GROUPS, SCAN_ROWS, 2 * LG_STATES), F32),
            pltpu.VMEM((SSM_LANE_GROUPS, SCAN_ROWS, 2 * LG_STATES), F32),
            pltpu.VMEM((SSM_LANE_GROUPS, 2, SCAN_CHUNKS, LG_STATES), F32),
            pltpu.VMEM((SCAN_ROWS, SSM_WIDTH), F32),
        ],
        compiler_params=pltpu.CompilerParams(
            dimension_semantics=("arbitrary", "arbitrary"), vmem_limit_bytes=40 * 1024 * 1024),
        name="ssm",
    )(u, perm, permt, wb, wc, a, apow, d, wglu, bglu)


def _ssm_params(a_re, a_im, log_dt, b_re, b_im, c_re, c_im):
    dt = jnp.exp(log_dt)[:, None]
    mag = jnp.exp(a_re * dt)
    ab_re = mag * jnp.cos(a_im * dt)
    ab_im = mag * jnp.sin(a_im * dt)
    nr = ab_re - 1.0
    ni = ab_im
    den = a_re * a_re + a_im * a_im
    z_re = (nr * a_re + ni * a_im) / den
    z_im = (ni * a_re - nr * a_im) / den
    bb_re = z_re[..., None] * b_re - z_im[..., None] * b_im
    bb_im = z_re[..., None] * b_im + z_im[..., None] * b_re

    gl = SSM_GROUPS // SSM_LANE_GROUPS
    eye = jnp.eye(gl, dtype=F32)

    def in_mat(bb):
        bb = bb.reshape(SSM_LANE_GROUPS, gl, SSM_STATE, SSM_GROUP)
        m = jnp.einsum('lgph,gk->lghkp', bb, eye)
        return m.reshape(SSM_LANE_GROUPS, LG_CHANNELS, LG_STATES)

    def out_mat(cc):
        cc = cc.reshape(SSM_LANE_GROUPS, gl, SSM_GROUP, SSM_STATE)
        m = jnp.einsum('lghp,gk->lgpkh', cc, eye)
        return m.reshape(SSM_LANE_GROUPS, LG_STATES, LG_CHANNELS)

    wb = jnp.concatenate([in_mat(bb_re), in_mat(bb_im)], axis=-1).astype(BF16)
    wc = jnp.concatenate([out_mat(c_re), out_mat(-c_im)], axis=-2).astype(BF16)

    pr, pi = ab_re, ab_im
    for _ in range(int(math.log2(SCAN_LEN))):
        pr, pi = pr * pr - pi * pi, 2.0 * pr * pi
    lanes = lambda v: v.reshape(SSM_LANE_GROUPS, LG_STATES)
    a = jnp.stack([lanes(ab_re), lanes(ab_im)], axis=1)
    apow = jnp.stack([lanes(pr), lanes(pi)], axis=1)
    return wb, wc, a, apow


def _merge_kernel(attn_ref, ssm_ref, ga_ref, gs_ref, x_ref, wba_ref, wbs_ref, wout_ref,
                  g_ref, x1_ref, h2_ref):
    a = _dot(attn_ref[...], wba_ref[...])
    s = _dot(ssm_ref[...], wbs_ref[...])
    merged = jax.nn.sigmoid(ga_ref[...]) * a + jax.nn.sigmoid(gs_ref[...]) * s
    x1 = x_ref[...] + _dot(merged.astype(BF16), wout_ref[...])
    x1_ref[...] = x1
    h2_ref[...] = _rms_norm(x1, g_ref[...]).astype(BF16)


def _merge(attn, ssm, ga, gs, x2, wba, wbs, wout, g, tm):
    t = x2.shape[0]
    row = lambda width: pl.BlockSpec((tm, width), lambda i: (i, 0))
    return pl.pallas_call(
        _merge_kernel,
        grid=(t // tm,),
        in_specs=[row(ATTN_WIDTH), row(SSM_WIDTH), row(D_MODEL), row(D_MODEL), row(D_MODEL),
                  _const_spec((ATTN_WIDTH, D_MODEL)), _const_spec((SSM_WIDTH, D_MODEL)),
                  _const_spec((D_MODEL, D_MODEL)), _const_spec((1, D_MODEL))],
        out_specs=[row(D_MODEL), row(D_MODEL)],
        out_shape=[jax.ShapeDtypeStruct((t, D_MODEL), F32),
                   jax.ShapeDtypeStruct((t, D_MODEL), BF16)],
        compiler_params=pltpu.CompilerParams(
            dimension_semantics=("arbitrary",), vmem_limit_bytes=56 * 1024 * 1024),
        name="merge",
    )(attn, ssm, ga, gs, x2, wba, wbs, wout, g)


FFN_TAIL = V7X_SUBLANES
FFN_CHUNK = 256


def _ffn_kernel(h_ref, x1_hbm, wv_ref, wg_ref, cw_ref, cb_ref, wd_ref, gf_ref,
                o_ref, x1_scr, act_scr, tail_scr, x1_sem, *, tm, tf, seq):
    i = pl.program_id(0)
    j = pl.program_id(1)
    last = pl.num_programs(1) - 1

    def x1_copy():
        return pltpu.make_async_copy(x1_hbm.at[pl.ds(i * tm, tm), :], x1_scr, x1_sem)

    @pl.when(jnp.logical_and(i == 0, j == 0))
    def _():
        tail_scr[...] = jnp.zeros_like(tail_scr)

    @pl.when(j == 0)
    def _():
        x1_copy().start()
        o_ref[...] = jnp.zeros_like(o_ref)

    seq_start = (i * tm) % seq == 0
    row = lax.broadcasted_iota(jnp.int32, (FFN_TAIL, FFN_CHUNK), 0)
    h = h_ref[...]
    for c in range(tf // FFN_CHUNK):
        cols = slice(c * FFN_CHUNK, (c + 1) * FFN_CHUNK)
        val = _dot(h, wv_ref[:, cols])
        gate = _dot(h, wg_ref[:, cols])
        prev = jnp.where(seq_start, 0.0, tail_scr[j, :, cols])
        tail_scr[j, :, cols] = gate[tm - FFN_TAIL:, :]
        conv = cb_ref[:, cols] + cw_ref[CONV_WIDTH - 1:CONV_WIDTH, cols] * gate
        for back in range(1, CONV_WIDTH):
            shifted = pltpu.roll(gate, back, 0)
            top = jnp.where(row < back, pltpu.roll(prev, back, 0), shifted[:FFN_TAIL])
            shifted = jnp.concatenate([top, shifted[FFN_TAIL:]], axis=0)
            k = CONV_WIDTH - 1 - back
            conv = conv + cw_ref[k:k + 1, cols] * shifted
        act_scr[:, cols] = (val * _gelu(conv)).astype(BF16)
    o_ref[...] += _dot(act_scr[...], wd_ref[...])

    @pl.when(j == last)
    def _():
        x1_copy().wait()
        o_ref[...] = _rms_norm(x1_scr[...] + o_ref[...], gf_ref[...])


def _ffn(h2, x1, w_up, conv_w, conv_b, w_down, gf, seq, tm, tf):
    t = h2.shape[0]
    nf = D_FF // tf
    kern = functools.partial(_ffn_kernel, tm=tm, tf=tf, seq=seq)
    return pl.pallas_call(
        kern,
        grid=(t // tm, nf),
        in_specs=[
            pl.BlockSpec((tm, D_MODEL), lambda i, j: (i, 0)),
            pl.BlockSpec(memory_space=pl.ANY),
            pl.BlockSpec((D_MODEL, tf), lambda i, j: (0, j)),
            pl.BlockSpec((D_MODEL, tf), lambda i, j: (0, j + nf)),
            pl.BlockSpec((CONV_WIDTH, tf), lambda i, j: (0, j)),
            pl.BlockSpec((1, tf), lambda i, j: (0, j)),
            pl.BlockSpec((tf, D_MODEL), lambda i, j: (j, 0)),
            _const_spec((1, D_MODEL)),
        ],
        out_specs=pl.BlockSpec((tm, D_MODEL), lambda i, j: (i, 0)),
        out_shape=jax.ShapeDtypeStruct((t, D_MODEL), F32),
        scratch_shapes=[pltpu.VMEM((tm, D_MODEL), F32),
                        pltpu.VMEM((tm, tf), BF16),
                        pltpu.VMEM((nf, FFN_TAIL, tf), F32),
                        pltpu.SemaphoreType.DMA(())],
        compiler_params=pltpu.CompilerParams(
            dimension_semantics=("arbitrary", "arbitrary"), vmem_limit_bytes=60 * 1024 * 1024),
        name="ffn",
    )(h2, x1, w_up, w_up, conv_w, conv_b, w_down, gf)


def _layer(layer, x, attn_norm_g, w_in, b_in, attn_sinks, ssm_a_re, ssm_a_im, ssm_log_dt,
           ssm_b_re, ssm_b_im, ssm_c_re, ssm_c_im, ssm_d, b_glu, ffn_norm_g, conv_w, conv_b,
           out_norm_g, stacked_weights):
    bsz, seq, _ = x.shape
    t = bsz * seq
    x2 = x.reshape(t, D_MODEL)
    row = lambda v: v.reshape(1, -1)

    (q, kv, u, ga, gs), (w_glu, w_branch_attn, w_branch_ssm, w_out, w_up, w_down) = _in_proj(
        x2, row(attn_norm_g), w_in.astype(BF16), row(b_in), stacked_weights, layer, tm=256)

    attn = _attention(q.reshape(bsz, seq, ATTN_WIDTH), kv.reshape(bsz, seq, 2 * KV_WIDTH),
                      _attn_bias_table(attn_sinks))

    wb, wc, a, apow = _ssm_params(ssm_a_re, ssm_a_im, ssm_log_dt, ssm_b_re, ssm_b_im,
                                  ssm_c_re, ssm_c_im)
    perm = _scan_permutation()
    ssm = _ssm(u.reshape(bsz, seq, SSM_WIDTH), jnp.asarray(perm, BF16), jnp.asarray(perm.T, BF16),
               wb, wc, a, apow, row(ssm_d), w_glu, row(b_glu))

    x1, h2 = _merge(attn.reshape(t, ATTN_WIDTH), ssm.reshape(t, SSM_WIDTH), ga, gs, x2,
                    w_branch_attn, w_branch_ssm, w_out, row(ffn_norm_g), tm=256)

    out = _ffn(h2, x1, w_up, conv_w, row(conv_b), w_down, row(out_norm_g),
               seq=seq, tm=1024, tf=512)
    return out.reshape(bsz, seq, D_MODEL)


def kernel(x, attn_norm_g, w_in, b_in, attn_sinks, ssm_a_re, ssm_a_im, ssm_log_dt, ssm_b_re,
           ssm_b_im, ssm_c_re, ssm_c_im, ssm_d, w_glu, b_glu, w_branch_attn, w_branch_ssm, w_out,
           ffn_norm_g, w_up, conv_w, conv_b, w_down, final_norm_g):
    depth = w_in.shape[0]
    assert depth == 1, "the fused final norm assumes a single layer"
    return _layer(0, x, attn_norm_g[0], w_in[0], b_in[0], attn_sinks[0], ssm_a_re[0], ssm_a_im[0],
                  ssm_log_dt[0], ssm_b_re[0], ssm_b_im[0], ssm_c_re[0], ssm_c_im[0], ssm_d[0],
                  b_glu[0], ffn_norm_g[0], conv_w[0], conv_b[0], final_norm_g,
                  [w_glu, w_branch_attn, w_branch_ssm, w_out, w_up, w_down])
```

```python
import functools
import math

import jax
import jax.numpy as jnp
import numpy as np
from jax import lax
from jax.experimental import pallas as pl
from jax.experimental.pallas import tpu as pltpu

D_MODEL = 2048
N_Q_HEADS = 16
N_KV_HEADS = 2
HEAD_DIM = 64
Q_PER_KV = N_Q_HEADS // N_KV_HEADS
WINDOW = 128
BLOCK = 128
ATTN_WIDTH = N_Q_HEADS * HEAD_DIM
KV_WIDTH = N_KV_HEADS * HEAD_DIM
SSM_GROUP = 16
SSM_GROUPS = 32
SSM_WIDTH = SSM_GROUP * SSM_GROUPS
SSM_STATE = 64
D_FF = 5632
CONV_WIDTH = 3
RMS_EPS = 1e-6
IN_COLS = ATTN_WIDTH + 2 * KV_WIDTH + SSM_WIDTH + 2 * D_MODEL
NEG_BIG = -1e30

V7X_LANES = 128
V7X_SUBLANES = 8
V7X_VMEM_BYTES = 64 * 1024 * 1024

BF16 = jnp.bfloat16
F32 = jnp.float32

_Q0 = 0
_K0 = ATTN_WIDTH
_V0 = _K0 + KV_WIDTH
_U0 = _V0 + KV_WIDTH
_GA0 = _U0 + SSM_WIDTH
_GS0 = _GA0 + D_MODEL

SCAN_CHUNKS = V7X_SUBLANES
SCAN_LEN = 64
SCAN_ROWS = SCAN_CHUNKS * SCAN_LEN
SSM_LANE_GROUPS = 4
LG_STATES = SSM_GROUPS * SSM_STATE // SSM_LANE_GROUPS
LG_CHANNELS = SSM_WIDTH // SSM_LANE_GROUPS


def _dot(a, b):
    return jnp.dot(a, b, preferred_element_type=F32)


def _dot_nt(a, b):
    return lax.dot_general(a, b, (((1,), (1,)), ((), ())), preferred_element_type=F32)


def _rms_norm(x, g):
    return x * lax.rsqrt(jnp.mean(x * x, axis=-1, keepdims=True) + RMS_EPS) * g


def _gelu(x):
    return 0.5 * x * (1.0 + lax.erf(x * np.float32(math.sqrt(0.5))))


def _const_spec(shape):
    nd = len(shape)
    return pl.BlockSpec(shape, lambda *_: (0,) * nd, pipeline_mode=pl.Buffered(1))


N_PROJ_OUT = 5
N_PROJ_SCRATCH = 3
W_IN_STAGE_ROWS = 128


def _in_proj_kernel(x_ref, g_ref, w_hbm, b_ref, *refs, layer):
    n_cast = (len(refs) - N_PROJ_OUT - N_PROJ_SCRATCH) // 2
    cast_in = refs[:n_cast]
    q_ref, kv_ref, u_ref, ga_ref, gs_ref = refs[n_cast:n_cast + N_PROJ_OUT]
    cast_out = refs[n_cast + N_PROJ_OUT:2 * n_cast + N_PROJ_OUT]
    w_scr, stage, sems = refs[2 * n_cast + N_PROJ_OUT:]

    @pl.when(pl.program_id(0) == 0)
    def _():
        n_slices = D_MODEL // W_IN_STAGE_ROWS

        def fetch(c):
            rows = pl.ds(c * W_IN_STAGE_ROWS, W_IN_STAGE_ROWS)
            return pltpu.make_async_copy(w_hbm.at[layer, rows, :], stage.at[c % 2], sems.at[c % 2])

        fetch(0).start()
        for c in range(n_slices):
            if c + 1 < n_slices:
                fetch(c + 1).start()
            fetch(c).wait()
            w_scr[c * W_IN_STAGE_ROWS:(c + 1) * W_IN_STAGE_ROWS, :] = stage[c % 2].astype(BF16)

    h = _rms_norm(x_ref[...], g_ref[...]).astype(BF16)
    proj = _dot(h, w_scr[...]) + b_ref[...]
    q_ref[...] = (proj[:, _Q0:_K0] * (HEAD_DIM ** -0.5)).astype(BF16)
    kv_ref[...] = proj[:, _K0:_U0].astype(BF16)
    u_ref[...] = proj[:, _U0:_GA0]
    ga_ref[...] = proj[:, _GA0:_GS0]
    gs_ref[...] = proj[:, _GS0:]
    for src, dst in zip(cast_in, cast_out):
        dst[...] = src[...].astype(BF16)


def _cast_tiling(n_rows, n_steps):
    bf16_rows = 2 * V7X_SUBLANES
    for rows in range(bf16_rows, n_rows + 1, bf16_rows):
        if n_rows % rows == 0 and n_rows // rows <= n_steps and n_steps % (n_rows // rows) == 0:
            return rows, n_steps // (n_rows // rows)
    raise ValueError(f"no cast tiling for {n_rows} rows over {n_steps} steps")


def _in_proj(x2, g, w, b, cast_weights, layer, tm):
    t = x2.shape[0]
    n_steps = t // tm
    row = lambda width: pl.BlockSpec((tm, width), lambda i: (i, 0))
    cast_in_specs, cast_out_specs, cast_shapes = [], [], []
    for cw in cast_weights:
        _, n_rows, n_cols = cw.shape
        rows, hold = _cast_tiling(n_rows, n_steps)
        cast_in_specs.append(pl.BlockSpec((None, rows, n_cols),
                                          lambda i, hold=hold: (layer, i // hold, 0)))
        cast_out_specs.append(pl.BlockSpec((rows, n_cols), lambda i, hold=hold: (i // hold, 0)))
        cast_shapes.append(jax.ShapeDtypeStruct((n_rows, n_cols), BF16))
    outs = pl.pallas_call(
        functools.partial(_in_proj_kernel, layer=layer),
        grid=(n_steps,),
        in_specs=[row(D_MODEL), _const_spec((1, D_MODEL)),
                  pl.BlockSpec(memory_space=pl.ANY), _const_spec((1, IN_COLS))] + cast_in_specs,
        out_specs=[row(ATTN_WIDTH), row(2 * KV_WIDTH), row(SSM_WIDTH),
                   row(D_MODEL), row(D_MODEL)] + cast_out_specs,
        out_shape=[jax.ShapeDtypeStruct((t, ATTN_WIDTH), BF16),
                   jax.ShapeDtypeStruct((t, 2 * KV_WIDTH), BF16),
                   jax.ShapeDtypeStruct((t, SSM_WIDTH), F32),
                   jax.ShapeDtypeStruct((t, D_MODEL), F32),
                   jax.ShapeDtypeStruct((t, D_MODEL), F32)] + cast_shapes,
        scratch_shapes=[pltpu.VMEM((D_MODEL, IN_COLS), BF16),
                        pltpu.VMEM((2, W_IN_STAGE_ROWS, IN_COLS), F32),
                        pltpu.SemaphoreType.DMA((2,))],
        compiler_params=pltpu.CompilerParams(
            dimension_semantics=("arbitrary",), vmem_limit_bytes=58 * 1024 * 1024),
        name="in_proj",
    )(x2, g, w, b, *cast_weights)
    return outs[:N_PROJ_OUT], outs[N_PROJ_OUT:]


ATTN_TQ = 1024
ATTN_SUB = ATTN_TQ // BLOCK


def _attn_bias_table(sinks):
    q_idx = np.arange(BLOCK)[:, None]
    s_idx = np.arange(2 * BLOCK)[None, :]
    dist = q_idx + BLOCK - s_idx
    band = (dist >= 0) & (dist < WINDOW)
    assert not band[:, 0].any()
    slopes = 2.0 ** (-8.0 * np.arange(1, N_Q_HEADS + 1, dtype=np.float32) / N_Q_HEADS)
    alibi = -slopes[:, None, None].astype(np.float32) * dist.astype(np.float32)[None]
    general = np.where(band[None], alibi, np.float32(NEG_BIG))
    first = np.where((band & (s_idx >= BLOCK))[None], alibi, np.float32(NEG_BIG))
    table = jnp.asarray(np.stack([general, first]).astype(np.float32))
    sink_col = jnp.asarray((s_idx == 0)[None, None])
    return jnp.where(sink_col, sinks.astype(F32)[None, :, None, None], table)


ATTN_PATCH = 16


def _attn_kernel(q_ref, kvc_ref, kvp_ref, bias_ref, o_ref, ops_scr):
    n = pl.program_id(1)
    lane = lax.broadcasted_iota(jnp.int32, (1, 2 * HEAD_DIM), 1)
    lo = lane < HEAD_DIM

    def operands(kv):
        kv = kv.astype(F32)
        k_nat = kv[:, :KV_WIDTH]
        v_nat = kv[:, KV_WIDTH:]
        k_swp = pltpu.roll(k_nat, HEAD_DIM, 1)
        v_swp = pltpu.roll(v_nat, HEAD_DIM, 1)
        return (jnp.where(lo, k_nat, 0.0), jnp.where(lo, k_swp, 0.0),
                jnp.where(lo, 0.0, k_swp), jnp.where(lo, 0.0, k_nat),
                jnp.where(lo, v_nat, 1.0), jnp.where(lo, v_swp, 1.0),
                jnp.where(lo, 1.0, v_swp), jnp.where(lo, 1.0, v_nat))

    for idx, op in enumerate(operands(kvp_ref[...])):
        ops_scr[idx, 0:BLOCK, :] = op.astype(BF16)
    for idx, op in enumerate(operands(kvc_ref[...])):
        ops_scr[idx, BLOCK:, :] = op.astype(BF16)

    sink_row = lax.broadcasted_iota(jnp.int32, (ATTN_PATCH, 1), 0) == 0
    zero = jnp.zeros((1, 2 * HEAD_DIM), F32)
    sink_fill = (zero, zero, zero, zero,
                 jnp.where(lo, 0.0, 1.0), jnp.where(lo, 0.0, 1.0),
                 jnp.where(lo, 1.0, 0.0), jnp.where(lo, 1.0, 0.0))

    def window(idx, r0):
        top = ops_scr[idx, pl.ds(r0, ATTN_PATCH), :].astype(F32)
        top = jnp.where(sink_row, sink_fill[idx], top).astype(BF16)
        rest = ops_scr[idx, pl.ds(r0 + ATTN_PATCH, 2 * BLOCK - ATTN_PATCH), :]
        return jnp.concatenate([top, rest], axis=0)

    def sub_block(sb, carry):
        r0 = pl.multiple_of(sb * BLOCK, BLOCK)
        first = jnp.where(jnp.logical_and(n == 0, sb == 0), 1, 0)
        win = [window(idx, r0) for idx in range(8)]
        q_all = q_ref[pl.ds(r0, BLOCK), :]
        pairs_per_group = Q_PER_KV // 2
        for grp in range(N_KV_HEADS):
            pairs = range(grp * pairs_per_group, (grp + 1) * pairs_per_group)
            q_stack = jnp.concatenate(
                [q_all[:, pr * 2 * HEAD_DIM:(pr + 1) * 2 * HEAD_DIM] for pr in pairs], axis=0)
            res = []
            for odd in range(2):
                bias = jnp.concatenate([bias_ref[first, 2 * pr + odd] for pr in pairs], axis=0)
                s = _dot_nt(q_stack, win[2 * odd + grp]) + bias
                m = jnp.max(s, axis=-1, keepdims=True)
                p = jnp.exp(s - m).astype(BF16)
                res.append(_dot(p, win[4 + 2 * odd + grp]))
            num = jnp.where(lo, res[0], res[1])
            den = pltpu.roll(jnp.where(lo, res[1], res[0]), HEAD_DIM, 1)
            out = (num / den).astype(BF16)
            for k, pr in enumerate(pairs):
                o_ref[pl.ds(r0, BLOCK), pr * 2 * HEAD_DIM:(pr + 1) * 2 * HEAD_DIM] = (
                    out[k * BLOCK:(k + 1) * BLOCK, :])
        return carry

    lax.fori_loop(0, ATTN_SUB, sub_block, 0)


def _attention(q, kv, bias):
    bsz, seq, _ = q.shape
    nq = seq // ATTN_TQ
    return pl.pallas_call(
        _attn_kernel,
        grid=(bsz, nq),
        in_specs=[
            pl.BlockSpec((None, ATTN_TQ, ATTN_WIDTH), lambda b, n: (b, n, 0)),
            pl.BlockSpec((None, ATTN_TQ, 2 * KV_WIDTH), lambda b, n: (b, n, 0)),
            pl.BlockSpec((None, BLOCK, 2 * KV_WIDTH),
                         lambda b, n: (b, jnp.maximum(n * ATTN_SUB - 1, 0), 0)),
            _const_spec((2, N_Q_HEADS, BLOCK, 2 * BLOCK)),
        ],
        out_specs=pl.BlockSpec((None, ATTN_TQ, ATTN_WIDTH), lambda b, n: (b, n, 0)),
        out_shape=jax.ShapeDtypeStruct((bsz, seq, ATTN_WIDTH), BF16),
        scratch_shapes=[pltpu.VMEM((8, ATTN_TQ + BLOCK, 2 * HEAD_DIM), BF16)],
        compiler_params=pltpu.CompilerParams(
            dimension_semantics=("arbitrary", "arbitrary"), vmem_limit_bytes=32 * 1024 * 1024),
        name="attn",
    )(q, kv, kv, bias)


def _scan_permutation():
    p = np.zeros((SCAN_ROWS, SCAN_ROWS), np.float32)
    for c in range(SCAN_CHUNKS):
        for s in range(SCAN_LEN):
            p[s * SCAN_CHUNKS + c, c * SCAN_LEN + s] = 1.0
    return p


def _ssm_kernel(u_ref, perm_ref, permt_ref, wb_ref, wc_ref, a_ref, apow_ref, d_ref,
                wglu_ref, bglu_ref, o_ref, carry_scr, bu_scr, xs_scr, x0_scr, y_scr, *, n_seq):
    @pl.when(pl.program_id(0) == 0)
    def _():
        carry_scr[...] = jnp.zeros_like(carry_scr)

    perm = perm_ref[...]
    u_perm, ub = [], []
    for b in range(n_seq):
        u = u_ref[b]
        u_hi = u.astype(BF16)
        u_lo = (u - u_hi.astype(F32)).astype(BF16)
        up_hi = _dot(perm, u_hi)
        u_perm.append(up_hi + _dot(perm, u_lo))
        ub.append(up_hi.astype(BF16))

    def project_in(b, lg):
        bu_scr[b, lg] = _dot(ub[b][:, lg * LG_CHANNELS:(lg + 1) * LG_CHANNELS], wb_ref[lg])

    def recurrence(b, lg, a_re, a_im, state, store):
        xr, xi = state
        for s in range(0, SCAN_LEN, 2):
            pair_re, pair_im = [], []
            for t in (s, s + 1):
                rows = slice(t * SCAN_CHUNKS, (t + 1) * SCAN_CHUNKS)
                br = bu_scr[b, lg, rows, :LG_STATES]
                bi = bu_scr[b, lg, rows, LG_STATES:]
                xr, xi = a_re * xr - a_im * xi + br, a_re * xi + a_im * xr + bi
                pair_re.append(xr)
                pair_im.append(xi)
            if store:
                rows = slice(s * SCAN_CHUNKS, (s + 2) * SCAN_CHUNKS)
                xs_scr[b, lg, rows, :LG_STATES] = jnp.concatenate(pair_re, axis=0).astype(BF16)
                xs_scr[b, lg, rows, LG_STATES:] = jnp.concatenate(pair_im, axis=0).astype(BF16)
        return xr, xi

    def scan(b, lg):
        a_re = jnp.broadcast_to(a_ref[lg, 0:1, :], (SCAN_CHUNKS, LG_STATES))
        a_im = jnp.broadcast_to(a_ref[lg, 1:2, :], (SCAN_CHUNKS, LG_STATES))
        zeros = jnp.zeros((SCAN_CHUNKS, LG_STATES), F32)
        e_re, e_im = recurrence(b, lg, a_re, a_im, (zeros, zeros), False)
        p_re = apow_ref[lg, 0:1, :]
        p_im = apow_ref[lg, 1:2, :]
        cur_re = carry_scr[b, lg, 0:1, :]
        cur_im = carry_scr[b, lg, 1:2, :]
        for c in range(SCAN_CHUNKS):
            x0_scr[b, lg, 0, c:c + 1, :] = cur_re
            x0_scr[b, lg, 1, c:c + 1, :] = cur_im
            nxt_re = p_re * cur_re - p_im * cur_im + e_re[c:c + 1, :]
            nxt_im = p_re * cur_im + p_im * cur_re + e_im[c:c + 1, :]
            cur_re, cur_im = nxt_re, nxt_im
        carry_scr[b, lg, 0:1, :] = cur_re
        carry_scr[b, lg, 1:2, :] = cur_im
        recurrence(b, lg, a_re, a_im, (x0_scr[b, lg, 0], x0_scr[b, lg, 1]), True)

    def project_out(b, lg):
        y_scr[b, :, lg * LG_CHANNELS:(lg + 1) * LG_CHANNELS] = _dot(xs_scr[b, lg], wc_ref[lg])

    for b in range(n_seq):
        project_in(b, 0)
    for lg in range(SSM_LANE_GROUPS):
        for b in range(n_seq):
            scan(b, lg)
            if lg + 1 < SSM_LANE_GROUPS:
                project_in(b, lg + 1)
            project_out(b, lg)

    for b in range(n_seq):
        y = y_scr[b] + d_ref[...] * u_perm[b]
        z = _dot(_gelu(y).astype(BF16), wglu_ref[...]) + bglu_ref[...]
        ssm = z[:, :SSM_WIDTH] * jax.nn.sigmoid(z[:, SSM_WIDTH:])
        o_ref[b] = _dot(permt_ref[...], ssm.astype(BF16)).astype(BF16)


def _ssm(u, perm, permt, wb, wc, a, apow, d, wglu, bglu):
    bsz, seq, _ = u.shape
    per_seq = lambda *shape, dt=F32: pltpu.VMEM((bsz,) + shape, dt)
    return pl.pallas_call(
        functools.partial(_ssm_kernel, n_seq=bsz),
        grid=(seq // SCAN_ROWS,),
        in_specs=[
            pl.BlockSpec((bsz, SCAN_ROWS, SSM_WIDTH), lambda s: (0, s, 0)),
            _const_spec((SCAN_ROWS, SCAN_ROWS)),
            _const_spec((SCAN_ROWS, SCAN_ROWS)),
            _const_spec((SSM_LANE_GROUPS, LG_CHANNELS, 2 * LG_STATES)),
            _const_spec((SSM_LANE_GROUPS, 2 * LG_STATES, LG_CHANNELS)),
            _const_spec((SSM_LANE_GROUPS, 2, LG_STATES)),
            _const_spec((SSM_LANE_GROUPS, 2, LG_STATES)),
            _const_spec((1, SSM_WIDTH)),
            _const_spec((SSM_WIDTH, 2 * SSM_WIDTH)),
            _const_spec((1, 2 * SSM_WIDTH)),
        ],
        out_specs=pl.BlockSpec((bsz, SCAN_ROWS, SSM_WIDTH), lambda s: (0, s, 0)),
        out_shape=jax.ShapeDtypeStruct((bsz, seq, SSM_WIDTH), BF16),
        scratch_shapes=[
            per_seq(SSM_LANE_GROUPS, 2, LG_STATES),
            per_seq(SSM_LANE_GROUPS, SCAN_ROWS, 2 * LG_STATES),
            per_seq(SSM_LANE_GROUPS, SCAN_ROWS, 2 * LG_STATES, dt=BF16),
            per_seq(SSM_LANE_GROUPS, 2, SCAN_CHUNKS, LG_STATES),
            per_seq(SCAN_ROWS, SSM_WIDTH),
        ],
        compiler_params=pltpu.CompilerParams(
            dimension_semantics=("arbitrary",), vmem_limit_bytes=48 * 1024 * 1024),
        name="ssm",
    )(u, perm, permt, wb, wc, a, apow, d, wglu, bglu)


def _ssm_params(a_re, a_im, log_dt, b_re, b_im, c_re, c_im):
    dt = jnp.exp(log_dt)[:, None]
    mag = jnp.exp(a_re * dt)
    ab_re = mag * jnp.cos(a_im * dt)
    ab_im = mag * jnp.sin(a_im * dt)
    nr = ab_re - 1.0
    ni = ab_im
    den = a_re * a_re + a_im * a_im
    z_re = (nr * a_re + ni * a_im) / den
    z_im = (ni * a_re - nr * a_im) / den
    bb_re = z_re[..., None] * b_re - z_im[..., None] * b_im
    bb_im = z_re[..., None] * b_im + z_im[..., None] * b_re

    gl = SSM_GROUPS // SSM_LANE_GROUPS
    eye = jnp.eye(gl, dtype=F32)

    def in_mat(bb):
        bb = bb.reshape(SSM_LANE_GROUPS, gl, SSM_STATE, SSM_GROUP)
        m = jnp.einsum('lgph,gk->lghkp', bb, eye)
        return m.reshape(SSM_LANE_GROUPS, LG_CHANNELS, LG_STATES)

    def out_mat(cc):
        cc = cc.reshape(SSM_LANE_GROUPS, gl, SSM_GROUP, SSM_STATE)
        m = jnp.einsum('lghp,gk->lgpkh', cc, eye)
        return m.reshape(SSM_LANE_GROUPS, LG_STATES, LG_CHANNELS)

    wb = jnp.concatenate([in_mat(bb_re), in_mat(bb_im)], axis=-1).astype(BF16)
    wc = jnp.concatenate([out_mat(c_re), out_mat(-c_im)], axis=-2).astype(BF16)

    pr, pi = ab_re, ab_im
    for _ in range(int(math.log2(SCAN_LEN))):
        pr, pi = pr * pr - pi * pi, 2.0 * pr * pi
    lanes = lambda v: v.reshape(SSM_LANE_GROUPS, LG_STATES)
    a = jnp.stack([lanes(ab_re), lanes(ab_im)], axis=1)
    apow = jnp.stack([lanes(pr), lanes(pi)], axis=1)
    return wb, wc, a, apow


def _merge_kernel(attn_ref, ssm_ref, ga_ref, gs_ref, x_ref, wba_ref, wbs_ref, wout_ref,
                  g_ref, x1_ref, h2_ref):
    a = _dot(attn_ref[...], wba_ref[...])
    s = _dot(ssm_ref[...], wbs_ref[...])
    merged = jax.nn.sigmoid(ga_ref[...]) * a + jax.nn.sigmoid(gs_ref[...]) * s
    x1 = x_ref[...] + _dot(merged.astype(BF16), wout_ref[...])
    x1_ref[...] = x1
    h2_ref[...] = _rms_norm(x1, g_ref[...]).astype(BF16)


def _merge(attn, ssm, ga, gs, x2, wba, wbs, wout, g, tm):
    t = x2.shape[0]
    row = lambda width: pl.BlockSpec((tm, width), lambda i: (i, 0))
    return pl.pallas_call(
        _merge_kernel,
        grid=(t // tm,),
        in_specs=[row(ATTN_WIDTH), row(SSM_WIDTH), row(D_MODEL), row(D_MODEL), row(D_MODEL),
                  _const_spec((ATTN_WIDTH, D_MODEL)), _const_spec((SSM_WIDTH, D_MODEL)),
                  _const_spec((D_MODEL, D_MODEL)), _const_spec((1, D_MODEL))],
        out_specs=[row(D_MODEL), row(D_MODEL)],
        out_shape=[jax.ShapeDtypeStruct((t, D_MODEL), F32),
                   jax.ShapeDtypeStruct((t, D_MODEL), BF16)],
        compiler_params=pltpu.CompilerParams(
            dimension_semantics=("arbitrary",), vmem_limit_bytes=56 * 1024 * 1024),
        name="merge",
    )(attn, ssm, ga, gs, x2, wba, wbs, wout, g)


FFN_TAIL = V7X_SUBLANES
FFN_CHUNK = 256


def _ffn_kernel(h_ref, x1_hbm, wv_ref, wg_ref, cw_ref, cb_ref, wd_ref, gf_ref,
                o_ref, x1_scr, act_scr, tail_scr, x1_sem, *, tm, tf, seq):
    i = pl.program_id(0)
    j = pl.program_id(1)
    last = pl.num_programs(1) - 1

    def x1_copy():
        return pltpu.make_async_copy(x1_hbm.at[pl.ds(i * tm, tm), :], x1_scr, x1_sem)

    @pl.when(jnp.logical_and(i == 0, j == 0))
    def _():
        tail_scr[...] = jnp.zeros_like(tail_scr)

    @pl.when(j == 0)
    def _():
        x1_copy().start()
        o_ref[...] = jnp.zeros_like(o_ref)

    seq_start = (i * tm) % seq == 0
    row = lax.broadcasted_iota(jnp.int32, (FFN_TAIL, FFN_CHUNK), 0)
    h = h_ref[...]
    for c in range(tf // FFN_CHUNK):
        cols = slice(c * FFN_CHUNK, (c + 1) * FFN_CHUNK)
        val = _dot(h, wv_ref[:, cols])
        gate = _dot(h, wg_ref[:, cols])
        prev = jnp.where(seq_start, 0.0, tail_scr[j, :, cols])
        tail_scr[j, :, cols] = gate[tm - FFN_TAIL:, :]
        conv = cb_ref[:, cols] + cw_ref[CONV_WIDTH - 1:CONV_WIDTH, cols] * gate
        for back in range(1, CONV_WIDTH):
            shifted = pltpu.roll(gate, back, 0)
            top = jnp.where(row < back, pltpu.roll(prev, back, 0), shifted[:FFN_TAIL])
            shifted = jnp.concatenate([top, shifted[FFN_TAIL:]], axis=0)
            k = CONV_WIDTH - 1 - back
            conv = conv + cw_ref[k:k + 1, cols] * shifted
        act_scr[:, cols] = (val * _gelu(conv)).astype(BF16)
    o_ref[...] += _dot(act_scr[...], wd_ref[...])

    @pl.when(j == last)
    def _():
        x1_copy().wait()
        o_ref[...] = _rms_norm(x1_scr[...] + o_ref[...], gf_ref[...])


def _ffn(h2, x1, w_up, conv_w, conv_b, w_down, gf, seq, tm, tf):
    t = h2.shape[0]
    nf = D_FF // tf
    kern = functools.partial(_ffn_kernel, tm=tm, tf=tf, seq=seq)
    return pl.pallas_call(
        kern,
        grid=(t // tm, nf),
        in_specs=[
            pl.BlockSpec((tm, D_MODEL), lambda i, j: (i, 0)),
            pl.BlockSpec(memory_space=pl.ANY),
            pl.BlockSpec((D_MODEL, tf), lambda i, j: (0, j)),
            pl.BlockSpec((D_MODEL, tf), lambda i, j: (0, j + nf)),
            pl.BlockSpec((CONV_WIDTH, tf), lambda i, j: (0, j)),
            pl.BlockSpec((1, tf), lambda i, j: (0, j)),
            pl.BlockSpec((tf, D_MODEL), lambda i, j: (j, 0)),
            _const_spec((1, D_MODEL)),
        ],
        out_specs=pl.BlockSpec((tm, D_MODEL), lambda i, j: (i, 0)),
        out_shape=jax.ShapeDtypeStruct((t, D_MODEL), F32),
        scratch_shapes=[pltpu.VMEM((tm, D_MODEL), F32),
                        pltpu.VMEM((tm, tf), BF16),
                        pltpu.VMEM((nf, FFN_TAIL, tf), F32),
                        pltpu.SemaphoreType.DMA(())],
        compiler_params=pltpu.CompilerParams(
            dimension_semantics=("arbitrary", "arbitrary"), vmem_limit_bytes=60 * 1024 * 1024),
        name="ffn",
    )(h2, x1, w_up, w_up, conv_w, conv_b, w_down, gf)


def _layer(layer, x, attn_norm_g, w_in, b_in, attn_sinks, ssm_a_re, ssm_a_im, ssm_log_dt,
           ssm_b_re, ssm_b_im, ssm_c_re, ssm_c_im, ssm_d, b_glu, ffn_norm_g, conv_w, conv_b,
           out_norm_g, stacked_weights):
    bsz, seq, _ = x.shape
    t = bsz * seq
    x2 = x.reshape(t, D_MODEL)
    row = lambda v: v.reshape(1, -1)

    (q, kv, u, ga, gs), (w_glu, w_branch_attn, w_branch_ssm, w_out, w_up, w_down) = _in_proj(
        x2, row(attn_norm_g), w_in, row(b_in), stacked_weights, layer, tm=256)

    attn = _attention(q.reshape(bsz, seq, ATTN_WIDTH), kv.reshape(bsz, seq, 2 * KV_WIDTH),
                      _attn_bias_table(attn_sinks))

    wb, wc, a, apow = _ssm_params(ssm_a_re, ssm_a_im, ssm_log_dt, ssm_b_re, ssm_b_im,
                                  ssm_c_re, ssm_c_im)
    perm = _scan_permutation()
    ssm = _ssm(u.reshape(bsz, seq, SSM_WIDTH), jnp.asarray(perm, BF16), jnp.asarray(perm.T, BF16),
               wb, wc, a, apow, row(ssm_d), w_glu, row(b_glu))

    x1, h2 = _merge(attn.reshape(t, ATTN_WIDTH), ssm.reshape(t, SSM_WIDTH), ga, gs, x2,
                    w_branch_attn, w_branch_ssm, w_out, row(ffn_norm_g), tm=256)

    out = _ffn(h2, x1, w_up, conv_w, row(conv_b), w_down, row(out_norm_g),
               seq=seq, tm=1024, tf=512)
    return out.reshape(bsz, seq, D_MODEL)


def kernel(x, attn_norm_g, w_in, b_in, attn_sinks, ssm_a_re, ssm_a_im, ssm_log_dt, ssm_b_re,
           ssm_b_im, ssm_c_re, ssm_c_im, ssm_d, w_glu, b_glu, w_branch_attn, w_branch_ssm, w_out,
           ffn_norm_g, w_up, conv_w, conv_b, w_down, final_norm_g):
    depth = w_in.shape[0]
    assert depth == 1, "the fused final norm assumes a single layer"
    return _layer(0, x, attn_norm_g[0], w_in, b_in[0], attn_sinks[0], ssm_a_re[0], ssm_a_im[0],
                  ssm_log_dt[0], ssm_b_re[0], ssm_b_im[0], ssm_c_re[0], ssm_c_im[0], ssm_d[0],
                  b_glu[0], ffn_norm_g[0], conv_w[0], conv_b[0], final_norm_g,
                  [w_glu, w_branch_attn, w_branch_ssm, w_out, w_up, w_down])
```

```python
import functools
import math

import jax
import jax.numpy as jnp
import numpy as np
from jax import lax
from jax.experimental import pallas as pl
from jax.experimental.pallas import tpu as pltpu

D_MODEL = 2048
N_Q_HEADS = 16
N_KV_HEADS = 2
HEAD_DIM = 64
Q_PER_KV = N_Q_HEADS // N_KV_HEADS
WINDOW = 128
BLOCK = 128
ATTN_WIDTH = N_Q_HEADS * HEAD_DIM
KV_WIDTH = N_KV_HEADS * HEAD_DIM
SSM_GROUP = 16
SSM_GROUPS = 32
SSM_WIDTH = SSM_GROUP * SSM_GROUPS
SSM_STATE = 64
D_FF = 5632
CONV_WIDTH = 3
RMS_EPS = 1e-6
IN_COLS = ATTN_WIDTH + 2 * KV_WIDTH + SSM_WIDTH + 2 * D_MODEL
NEG_BIG = -1e30

V7X_LANES = 128
V7X_SUBLANES = 8
V7X_VMEM_BYTES = 64 * 1024 * 1024
MIB = 1024 * 1024

BF16 = jnp.bfloat16
F32 = jnp.float32

IN_PROJ_ROWS = 256
MERGE_ROWS = 256
FFN_ROWS = 1024
FFN_DFF_TILE = 512
IN_PROJ_VMEM = 58 * MIB
ATTN_VMEM = 32 * MIB
SSM_VMEM = 48 * MIB
MERGE_VMEM = 56 * MIB
FFN_VMEM = 60 * MIB
assert max(IN_PROJ_VMEM, ATTN_VMEM, SSM_VMEM, MERGE_VMEM, FFN_VMEM) < V7X_VMEM_BYTES

_Q0 = 0
_K0 = ATTN_WIDTH
_V0 = _K0 + KV_WIDTH
_U0 = _V0 + KV_WIDTH
_GA0 = _U0 + SSM_WIDTH
_GS0 = _GA0 + D_MODEL

SCAN_CHUNKS = V7X_SUBLANES
SCAN_LEN = 64
SCAN_ROWS = SCAN_CHUNKS * SCAN_LEN
SSM_LANE_GROUPS = 4
LG_STATES = SSM_GROUPS * SSM_STATE // SSM_LANE_GROUPS
LG_CHANNELS = SSM_WIDTH // SSM_LANE_GROUPS


def _dot(a, b):
    return jnp.dot(a, b, preferred_element_type=F32)


def _dot_nt(a, b):
    return lax.dot_general(a, b, (((1,), (1,)), ((), ())), preferred_element_type=F32)


def _rms_norm(x, g):
    return x * lax.rsqrt(jnp.mean(x * x, axis=-1, keepdims=True) + RMS_EPS) * g


def _gelu(x):
    return 0.5 * x * (1.0 + lax.erf(x * np.float32(math.sqrt(0.5))))


def _const_spec(shape):
    nd = len(shape)
    return pl.BlockSpec(shape, lambda *_: (0,) * nd, pipeline_mode=pl.Buffered(1))


N_PROJ_OUT = 5
N_PROJ_SCRATCH = 3
W_IN_STAGE_ROWS = 128


def _in_proj_kernel(x_ref, g_ref, w_hbm, b_ref, *refs, layer):
    n_cast = (len(refs) - N_PROJ_OUT - N_PROJ_SCRATCH) // 2
    cast_in = refs[:n_cast]
    q_ref, kv_ref, u_ref, ga_ref, gs_ref = refs[n_cast:n_cast + N_PROJ_OUT]
    cast_out = refs[n_cast + N_PROJ_OUT:2 * n_cast + N_PROJ_OUT]
    w_scr, stage, sems = refs[2 * n_cast + N_PROJ_OUT:]

    @pl.when(pl.program_id(0) == 0)
    def _():
        n_slices = D_MODEL // W_IN_STAGE_ROWS

        def fetch(c):
            rows = pl.ds(c * W_IN_STAGE_ROWS, W_IN_STAGE_ROWS)
            return pltpu.make_async_copy(w_hbm.at[layer, rows, :], stage.at[c % 2], sems.at[c % 2])

        fetch(0).start()
        for c in range(n_slices):
            if c + 1 < n_slices:
                fetch(c + 1).start()
            fetch(c).wait()
            w_scr[c * W_IN_STAGE_ROWS:(c + 1) * W_IN_STAGE_ROWS, :] = stage[c % 2].astype(BF16)

    h = _rms_norm(x_ref[...], g_ref[...]).astype(BF16)
    proj = _dot(h, w_scr[...]) + b_ref[...]
    q_ref[...] = (proj[:, _Q0:_K0] * (HEAD_DIM ** -0.5)).astype(BF16)
    kv_ref[...] = proj[:, _K0:_U0].astype(BF16)
    u_ref[...] = proj[:, _U0:_GA0]
    ga_ref[...] = proj[:, _GA0:_GS0]
    gs_ref[...] = proj[:, _GS0:]
    for src, dst in zip(cast_in, cast_out):
        dst[...] = src[...].astype(BF16)


def _cast_tiling(n_rows, n_steps):
    bf16_rows = 2 * V7X_SUBLANES
    for rows in range(bf16_rows, n_rows + 1, bf16_rows):
        if n_rows % rows == 0 and n_rows // rows <= n_steps and n_steps % (n_rows // rows) == 0:
            return rows, n_steps // (n_rows // rows)
    raise ValueError(f"no cast tiling for {n_rows} rows over {n_steps} steps")


def _in_proj(x2, g, w, b, cast_weights, layer, tm):
    t = x2.shape[0]
    n_steps = t // tm
    row = lambda width: pl.BlockSpec((tm, width), lambda i: (i, 0))
    cast_in_specs, cast_out_specs, cast_shapes = [], [], []
    for cw in cast_weights:
        _, n_rows, n_cols = cw.shape
        rows, hold = _cast_tiling(n_rows, n_steps)
        cast_in_specs.append(pl.BlockSpec((None, rows, n_cols),
                                          lambda i, hold=hold: (layer, i // hold, 0)))
        cast_out_specs.append(pl.BlockSpec((rows, n_cols), lambda i, hold=hold: (i // hold, 0)))
        cast_shapes.append(jax.ShapeDtypeStruct((n_rows, n_cols), BF16))
    outs = pl.pallas_call(
        functools.partial(_in_proj_kernel, layer=layer),
        grid=(n_steps,),
        in_specs=[row(D_MODEL), _const_spec((1, D_MODEL)),
                  pl.BlockSpec(memory_space=pl.ANY), _const_spec((1, IN_COLS))] + cast_in_specs,
        out_specs=[row(ATTN_WIDTH), row(2 * KV_WIDTH), row(SSM_WIDTH),
                   row(D_MODEL), row(D_MODEL)] + cast_out_specs,
        out_shape=[jax.ShapeDtypeStruct((t, ATTN_WIDTH), BF16),
                   jax.ShapeDtypeStruct((t, 2 * KV_WIDTH), BF16),
                   jax.ShapeDtypeStruct((t, SSM_WIDTH), F32),
                   jax.ShapeDtypeStruct((t, D_MODEL), F32),
                   jax.ShapeDtypeStruct((t, D_MODEL), F32)] + cast_shapes,
        scratch_shapes=[pltpu.VMEM((D_MODEL, IN_COLS), BF16),
                        pltpu.VMEM((2, W_IN_STAGE_ROWS, IN_COLS), F32),
                        pltpu.SemaphoreType.DMA((2,))],
        compiler_params=pltpu.CompilerParams(
            dimension_semantics=("arbitrary",), vmem_limit_bytes=IN_PROJ_VMEM),
        name="in_proj",
    )(x2, g, w, b, *cast_weights)
    return outs[:N_PROJ_OUT], outs[N_PROJ_OUT:]


ATTN_TQ = 1024
ATTN_SUB = ATTN_TQ // BLOCK


def _attn_bias_table(sinks):
    q_idx = np.arange(BLOCK)[:, None]
    s_idx = np.arange(2 * BLOCK)[None, :]
    dist = q_idx + BLOCK - s_idx
    band = (dist >= 0) & (dist < WINDOW)
    assert not band[:, 0].any()
    slopes = 2.0 ** (-8.0 * np.arange(1, N_Q_HEADS + 1, dtype=np.float32) / N_Q_HEADS)
    alibi = -slopes[:, None, None].astype(np.float32) * dist.astype(np.float32)[None]
    general = np.where(band[None], alibi, np.float32(NEG_BIG))
    first = np.where((band & (s_idx >= BLOCK))[None], alibi, np.float32(NEG_BIG))
    table = jnp.asarray(np.stack([general, first]).astype(np.float32))
    sink_col = jnp.asarray((s_idx == 0)[None, None])
    return jnp.where(sink_col, sinks.astype(F32)[None, :, None, None], table)


ATTN_PATCH = 16


def _attn_kernel(q_ref, kvc_ref, kvp_ref, bias_ref, o_ref, ops_scr):
    n = pl.program_id(1)
    lane = lax.broadcasted_iota(jnp.int32, (1, 2 * HEAD_DIM), 1)
    lo = lane < HEAD_DIM

    def operands(kv):
        kv = kv.astype(F32)
        k_nat = kv[:, :KV_WIDTH]
        v_nat = kv[:, KV_WIDTH:]
        k_swp = pltpu.roll(k_nat, HEAD_DIM, 1)
        v_swp = pltpu.roll(v_nat, HEAD_DIM, 1)
        return (jnp.where(lo, k_nat, 0.0), jnp.where(lo, k_swp, 0.0),
                jnp.where(lo, 0.0, k_swp), jnp.where(lo, 0.0, k_nat),
                jnp.where(lo, v_nat, 1.0), jnp.where(lo, v_swp, 1.0),
                jnp.where(lo, 1.0, v_swp), jnp.where(lo, 1.0, v_nat))

    for idx, op in enumerate(operands(kvp_ref[...])):
        ops_scr[idx, 0:BLOCK, :] = op.astype(BF16)
    for idx, op in enumerate(operands(kvc_ref[...])):
        ops_scr[idx, BLOCK:, :] = op.astype(BF16)

    sink_row = lax.broadcasted_iota(jnp.int32, (ATTN_PATCH, 1), 0) == 0
    zero = jnp.zeros((1, 2 * HEAD_DIM), F32)
    sink_fill = (zero, zero, zero, zero,
                 jnp.where(lo, 0.0, 1.0), jnp.where(lo, 0.0, 1.0),
                 jnp.where(lo, 1.0, 0.0), jnp.where(lo, 1.0, 0.0))

    def window(idx, r0):
        top = ops_scr[idx, pl.ds(r0, ATTN_PATCH), :].astype(F32)
        top = jnp.where(sink_row, sink_fill[idx], top).astype(BF16)
        rest = ops_scr[idx, pl.ds(r0 + ATTN_PATCH, 2 * BLOCK - ATTN_PATCH), :]
        return jnp.concatenate([top, rest], axis=0)

    def sub_block(sb, carry):
        r0 = pl.multiple_of(sb * BLOCK, BLOCK)
        first = jnp.where(jnp.logical_and(n == 0, sb == 0), 1, 0)
        win = [window(idx, r0) for idx in range(8)]
        q_all = q_ref[pl.ds(r0, BLOCK), :]
        pairs_per_group = Q_PER_KV // 2
        for grp in range(N_KV_HEADS):
            pairs = range(grp * pairs_per_group, (grp + 1) * pairs_per_group)
            q_stack = jnp.concatenate(
                [q_all[:, pr * 2 * HEAD_DIM:(pr + 1) * 2 * HEAD_DIM] for pr in pairs], axis=0)
            res = []
            for odd in range(2):
                bias = jnp.concatenate([bias_ref[first, 2 * pr + odd] for pr in pairs], axis=0)
                s = _dot_nt(q_stack, win[2 * odd + grp]) + bias
                m = jnp.max(s, axis=-1, keepdims=True)
                p = jnp.exp(s - m).astype(BF16)
                res.append(_dot(p, win[4 + 2 * odd + grp]))
            num = jnp.where(lo, res[0], res[1])
            den = pltpu.roll(jnp.where(lo, res[1], res[0]), HEAD_DIM, 1)
            out = (num / den).astype(BF16)
            for k, pr in enumerate(pairs):
                o_ref[pl.ds(r0, BLOCK), pr * 2 * HEAD_DIM:(pr + 1) * 2 * HEAD_DIM] = (
                    out[k * BLOCK:(k + 1) * BLOCK, :])
        return carry

    lax.fori_loop(0, ATTN_SUB, sub_block, 0)


def _attention(q, kv, bias):
    bsz, seq, _ = q.shape
    nq = seq // ATTN_TQ
    return pl.pallas_call(
        _attn_kernel,
        grid=(bsz, nq),
        in_specs=[
            pl.BlockSpec((None, ATTN_TQ, ATTN_WIDTH), lambda b, n: (b, n, 0)),
            pl.BlockSpec((None, ATTN_TQ, 2 * KV_WIDTH), lambda b, n: (b, n, 0)),
            pl.BlockSpec((None, BLOCK, 2 * KV_WIDTH),
                         lambda b, n: (b, jnp.maximum(n * ATTN_SUB - 1, 0), 0)),
            _const_spec((2, N_Q_HEADS, BLOCK, 2 * BLOCK)),
        ],
        out_specs=pl.BlockSpec((None, ATTN_TQ, ATTN_WIDTH), lambda b, n: (b, n, 0)),
        out_shape=jax.ShapeDtypeStruct((bsz, seq, ATTN_WIDTH), BF16),
        scratch_shapes=[pltpu.VMEM((8, ATTN_TQ + BLOCK, 2 * HEAD_DIM), BF16)],
        compiler_params=pltpu.CompilerParams(
            dimension_semantics=("arbitrary", "arbitrary"), vmem_limit_bytes=ATTN_VMEM),
        name="attn",
    )(q, kv, kv, bias)


def _scan_permutation():
    p = np.zeros((SCAN_ROWS, SCAN_ROWS), np.float32)
    for c in range(SCAN_CHUNKS):
        for s in range(SCAN_LEN):
            p[s * SCAN_CHUNKS + c, c * SCAN_LEN + s] = 1.0
    return p


def _ssm_kernel(u_ref, perm_ref, permt_ref, wb_ref, wc_ref, a_ref, apow_ref, d_ref,
                wglu_ref, bglu_ref, o_ref, carry_scr, bu_scr, xs_scr, x0_scr, y_scr, *, n_seq):
    @pl.when(pl.program_id(0) == 0)
    def _():
        carry_scr[...] = jnp.zeros_like(carry_scr)

    perm = perm_ref[...]
    u_perm, ub = {}, {}

    def reorder(b):
        u = u_ref[b]
        u_hi = u.astype(BF16)
        u_lo = (u - u_hi.astype(F32)).astype(BF16)
        up_hi = _dot(perm, u_hi)
        u_perm[b] = up_hi + _dot(perm, u_lo)
        ub[b] = up_hi.astype(BF16)

    def project_in(b, lg):
        bu_scr[b, lg] = _dot(ub[b][:, lg * LG_CHANNELS:(lg + 1) * LG_CHANNELS], wb_ref[lg])

    def recurrence(b, lg, a_re, a_im, state, store):
        xr, xi = state
        for s in range(0, SCAN_LEN, 2):
            pair_re, pair_im = [], []
            for t in (s, s + 1):
                rows = slice(t * SCAN_CHUNKS, (t + 1) * SCAN_CHUNKS)
                br = bu_scr[b, lg, rows, :LG_STATES]
                bi = bu_scr[b, lg, rows, LG_STATES:]
                xr, xi = a_re * xr - a_im * xi + br, a_re * xi + a_im * xr + bi
                pair_re.append(xr)
                pair_im.append(xi)
            if store:
                rows = slice(s * SCAN_CHUNKS, (s + 2) * SCAN_CHUNKS)
                xs_scr[b, lg, rows, :LG_STATES] = jnp.concatenate(pair_re, axis=0).astype(BF16)
                xs_scr[b, lg, rows, LG_STATES:] = jnp.concatenate(pair_im, axis=0).astype(BF16)
        return xr, xi

    def scan(b, lg):
        a_re = jnp.broadcast_to(a_ref[lg, 0:1, :], (SCAN_CHUNKS, LG_STATES))
        a_im = jnp.broadcast_to(a_ref[lg, 1:2, :], (SCAN_CHUNKS, LG_STATES))
        zeros = jnp.zeros((SCAN_CHUNKS, LG_STATES), F32)
        e_re, e_im = recurrence(b, lg, a_re, a_im, (zeros, zeros), False)
        p_re = apow_ref[lg, 0:1, :]
        p_im = apow_ref[lg, 1:2, :]
        cur_re = carry_scr[b, lg, 0:1, :]
        cur_im = carry_scr[b, lg, 1:2, :]
        for c in range(SCAN_CHUNKS):
            x0_scr[b, lg, 0, c:c + 1, :] = cur_re
            x0_scr[b, lg, 1, c:c + 1, :] = cur_im
            nxt_re = p_re * cur_re - p_im * cur_im + e_re[c:c + 1, :]
            nxt_im = p_re * cur_im + p_im * cur_re + e_im[c:c + 1, :]
            cur_re, cur_im = nxt_re, nxt_im
        carry_scr[b, lg, 0:1, :] = cur_re
        carry_scr[b, lg, 1:2, :] = cur_im
        recurrence(b, lg, a_re, a_im, (x0_scr[b, lg, 0], x0_scr[b, lg, 1]), True)

    def project_out(b, lg):
        y_scr[b, :, lg * LG_CHANNELS:(lg + 1) * LG_CHANNELS] = _dot(xs_scr[b, lg], wc_ref[lg])

    def finish(b):
        y = y_scr[b] + d_ref[...] * u_perm[b]
        z = _dot(_gelu(y).astype(BF16), wglu_ref[...]) + bglu_ref[...]
        ssm = z[:, :SSM_WIDTH] * jax.nn.sigmoid(z[:, SSM_WIDTH:])
        o_ref[b] = _dot(permt_ref[...], ssm.astype(BF16)).astype(BF16)

    reorder(0)
    project_in(0, 0)
    for lg in range(SSM_LANE_GROUPS):
        for b in range(n_seq):
            scan(b, lg)
            if lg == 0 and b + 1 < n_seq:
                reorder(b + 1)
                project_in(b + 1, 0)
            if lg + 1 < SSM_LANE_GROUPS:
                project_in(b, lg + 1)
            project_out(b, lg)
            if lg + 1 == SSM_LANE_GROUPS:
                finish(b)


def _ssm(u, perm, permt, wb, wc, a, apow, d, wglu, bglu):
    bsz, seq, _ = u.shape
    per_seq = lambda *shape, dt=F32: pltpu.VMEM((bsz,) + shape, dt)
    return pl.pallas_call(
        functools.partial(_ssm_kernel, n_seq=bsz),
        grid=(seq // SCAN_ROWS,),
        in_specs=[
            pl.BlockSpec((bsz, SCAN_ROWS, SSM_WIDTH), lambda s: (0, s, 0)),
            _const_spec((SCAN_ROWS, SCAN_ROWS)),
            _const_spec((SCAN_ROWS, SCAN_ROWS)),
            _const_spec((SSM_LANE_GROUPS, LG_CHANNELS, 2 * LG_STATES)),
            _const_spec((SSM_LANE_GROUPS, 2 * LG_STATES, LG_CHANNELS)),
            _const_spec((SSM_LANE_GROUPS, 2, LG_STATES)),
            _const_spec((SSM_LANE_GROUPS, 2, LG_STATES)),
            _const_spec((1, SSM_WIDTH)),
            _const_spec((SSM_WIDTH, 2 * SSM_WIDTH)),
            _const_spec((1, 2 * SSM_WIDTH)),
        ],
        out_specs=pl.BlockSpec((bsz, SCAN_ROWS, SSM_WIDTH), lambda s: (0, s, 0)),
        out_shape=jax.ShapeDtypeStruct((bsz, seq, SSM_WIDTH), BF16),
        scratch_shapes=[
            per_seq(SSM_LANE_GROUPS, 2, LG_STATES),
            per_seq(SSM_LANE_GROUPS, SCAN_ROWS, 2 * LG_STATES),
            per_seq(SSM_LANE_GROUPS, SCAN_ROWS, 2 * LG_STATES, dt=BF16),
            per_seq(SSM_LANE_GROUPS, 2, SCAN_CHUNKS, LG_STATES),
            per_seq(SCAN_ROWS, SSM_WIDTH),
        ],
        compiler_params=pltpu.CompilerParams(
            dimension_semantics=("arbitrary",), vmem_limit_bytes=SSM_VMEM),
        name="ssm",
    )(u, perm, permt, wb, wc, a, apow, d, wglu, bglu)


def _ssm_params(a_re, a_im, log_dt, b_re, b_im, c_re, c_im):
    dt = jnp.exp(log_dt)[:, None]
    mag = jnp.exp(a_re * dt)
    ab_re = mag * jnp.cos(a_im * dt)
    ab_im = mag * jnp.sin(a_im * dt)
    nr = ab_re - 1.0
    ni = ab_im
    den = a_re * a_re + a_im * a_im
    z_re = (nr * a_re + ni * a_im) / den
    z_im = (ni * a_re - nr * a_im) / den
    bb_re = z_re[..., None] * b_re - z_im[..., None] * b_im
    bb_im = z_re[..., None] * b_im + z_im[..., None] * b_re

    gl = SSM_GROUPS // SSM_LANE_GROUPS
    eye = jnp.eye(gl, dtype=F32)

    def in_mat(bb):
        bb = bb.reshape(SSM_LANE_GROUPS, gl, SSM_STATE, SSM_GROUP)
        m = jnp.einsum('lgph,gk->lghkp', bb, eye)
        return m.reshape(SSM_LANE_GROUPS, LG_CHANNELS, LG_STATES)

    def out_mat(cc):
        cc = cc.reshape(SSM_LANE_GROUPS, gl, SSM_GROUP, SSM_STATE)
        m = jnp.einsum('lghp,gk->lgpkh', cc, eye)
        return m.reshape(SSM_LANE_GROUPS, LG_STATES, LG_CHANNELS)

    wb = jnp.concatenate([in_mat(bb_re), in_mat(bb_im)], axis=-1).astype(BF16)
    wc = jnp.concatenate([out_mat(c_re), out_mat(-c_im)], axis=-2).astype(BF16)

    pr, pi = ab_re, ab_im
    for _ in range(int(math.log2(SCAN_LEN))):
        pr, pi = pr * pr - pi * pi, 2.0 * pr * pi
    lanes = lambda v: v.reshape(SSM_LANE_GROUPS, LG_STATES)
    a = jnp.stack([lanes(ab_re), lanes(ab_im)], axis=1)
    apow = jnp.stack([lanes(pr), lanes(pi)], axis=1)
    return wb, wc, a, apow


def _merge_kernel(attn_ref, ssm_ref, ga_ref, gs_ref, x_ref, wba_ref, wbs_ref, wout_ref,
                  g_ref, x1_ref, h2_ref):
    a = _dot(attn_ref[...], wba_ref[...])
    s = _dot(ssm_ref[...], wbs_ref[...])
    merged = jax.nn.sigmoid(ga_ref[...]) * a + jax.nn.sigmoid(gs_ref[...]) * s
    x1 = x_ref[...] + _dot(merged.astype(BF16), wout_ref[...])
    x1_ref[...] = x1
    h2_ref[...] = _rms_norm(x1, g_ref[...]).astype(BF16)


def _merge(attn, ssm, ga, gs, x2, wba, wbs, wout, g, tm):
    t = x2.shape[0]
    row = lambda width: pl.BlockSpec((tm, width), lambda i: (i, 0))
    return pl.pallas_call(
        _merge_kernel,
        grid=(t // tm,),
        in_specs=[row(ATTN_WIDTH), row(SSM_WIDTH), row(D_MODEL), row(D_MODEL), row(D_MODEL),
                  _const_spec((ATTN_WIDTH, D_MODEL)), _const_spec((SSM_WIDTH, D_MODEL)),
                  _const_spec((D_MODEL, D_MODEL)), _const_spec((1, D_MODEL))],
        out_specs=[row(D_MODEL), row(D_MODEL)],
        out_shape=[jax.ShapeDtypeStruct((t, D_MODEL), F32),
                   jax.ShapeDtypeStruct((t, D_MODEL), BF16)],
        compiler_params=pltpu.CompilerParams(
            dimension_semantics=("arbitrary",), vmem_limit_bytes=MERGE_VMEM),
        name="merge",
    )(attn, ssm, ga, gs, x2, wba, wbs, wout, g)


FFN_TAIL = V7X_SUBLANES
FFN_CHUNK = 256


def _ffn_kernel(h_ref, x1_hbm, wv_ref, wg_ref, cw_ref, cb_ref, wd_ref, gf_ref,
                o_ref, x1_scr, act_scr, tail_scr, x1_sem, *, tm, tf, seq):
    i = pl.program_id(0)
    j = pl.program_id(1)
    last = pl.num_programs(1) - 1

    def x1_copy():
        return pltpu.make_async_copy(x1_hbm.at[pl.ds(i * tm, tm), :], x1_scr, x1_sem)

    @pl.when(jnp.logical_and(i == 0, j == 0))
    def _():
        tail_scr[...] = jnp.zeros_like(tail_scr)

    @pl.when(j == 0)
    def _():
        x1_copy().start()
        o_ref[...] = jnp.zeros_like(o_ref)

    seq_start = (i * tm) % seq == 0
    row = lax.broadcasted_iota(jnp.int32, (FFN_TAIL, FFN_CHUNK), 0)
    h = h_ref[...]
    for c in range(tf // FFN_CHUNK):
        cols = slice(c * FFN_CHUNK, (c + 1) * FFN_CHUNK)
        val = _dot(h, wv_ref[:, cols])
        gate = _dot(h, wg_ref[:, cols])
        prev = jnp.where(seq_start, 0.0, tail_scr[j, :, cols])
        tail_scr[j, :, cols] = gate[tm - FFN_TAIL:, :]
        conv = cb_ref[:, cols] + cw_ref[CONV_WIDTH - 1:CONV_WIDTH, cols] * gate
        for back in range(1, CONV_WIDTH):
            shifted = pltpu.roll(gate, back, 0)
            top = jnp.where(row < back, pltpu.roll(prev, back, 0), shifted[:FFN_TAIL])
            shifted = jnp.concatenate([top, shifted[FFN_TAIL:]], axis=0)
            k = CONV_WIDTH - 1 - back
            conv = conv + cw_ref[k:k + 1, cols] * shifted
        act_scr[:, cols] = (val * _gelu(conv)).astype(BF16)
    o_ref[...] += _dot(act_scr[...], wd_ref[...])

    @pl.when(j == last)
    def _():
        x1_copy().wait()
        o_ref[...] = _rms_norm(x1_scr[...] + o_ref[...], gf_ref[...])


def _ffn(h2, x1, w_up, conv_w, conv_b, w_down, gf, seq, tm, tf):
    t = h2.shape[0]
    nf = D_FF // tf
    kern = functools.partial(_ffn_kernel, tm=tm, tf=tf, seq=seq)
    return pl.pallas_call(
        kern,
        grid=(t // tm, nf),
        in_specs=[
            pl.BlockSpec((tm, D_MODEL), lambda i, j: (i, 0)),
            pl.BlockSpec(memory_space=pl.ANY),
            pl.BlockSpec((D_MODEL, tf), lambda i, j: (0, j)),
            pl.BlockSpec((D_MODEL, tf), lambda i, j: (0, j + nf)),
            pl.BlockSpec((CONV_WIDTH, tf), lambda i, j: (0, j)),
            pl.BlockSpec((1, tf), lambda i, j: (0, j)),
            pl.BlockSpec((tf, D_MODEL), lambda i, j: (j, 0)),
            _const_spec((1, D_MODEL)),
        ],
        out_specs=pl.BlockSpec((tm, D_MODEL), lambda i, j: (i, 0)),
        out_shape=jax.ShapeDtypeStruct((t, D_MODEL), F32),
        scratch_shapes=[pltpu.VMEM((tm, D_MODEL), F32),
                        pltpu.VMEM((tm, tf), BF16),
                        pltpu.VMEM((nf, FFN_TAIL, tf), F32),
                        pltpu.SemaphoreType.DMA(())],
        compiler_params=pltpu.CompilerParams(
            dimension_semantics=("arbitrary", "arbitrary"), vmem_limit_bytes=FFN_VMEM),
        name="ffn",
    )(h2, x1, w_up, w_up, conv_w, conv_b, w_down, gf)


def _layer(layer, x, attn_norm_g, w_in, b_in, attn_sinks, ssm_a_re, ssm_a_im, ssm_log_dt,
           ssm_b_re, ssm_b_im, ssm_c_re, ssm_c_im, ssm_d, b_glu, ffn_norm_g, conv_w, conv_b,
           out_norm_g, stacked_weights):
    bsz, seq, width = x.shape
    t = bsz * seq
    assert width == D_MODEL and x.dtype == F32
    assert seq % ATTN_TQ == 0 and seq % SCAN_ROWS == 0 and seq % FFN_ROWS == 0
    assert t % IN_PROJ_ROWS == 0 and t % MERGE_ROWS == 0
    assert D_FF % FFN_DFF_TILE == 0 and FFN_DFF_TILE % FFN_CHUNK == 0
    x2 = x.reshape(t, D_MODEL)
    row = lambda v: v.reshape(1, -1)

    (q, kv, u, ga, gs), (w_glu, w_branch_attn, w_branch_ssm, w_out, w_up, w_down) = _in_proj(
        x2, row(attn_norm_g), w_in, row(b_in), stacked_weights, layer, tm=IN_PROJ_ROWS)

    attn = _attention(q.reshape(bsz, seq, ATTN_WIDTH), kv.reshape(bsz, seq, 2 * KV_WIDTH),
                      _attn_bias_table(attn_sinks))

    wb, wc, a, apow = _ssm_params(ssm_a_re, ssm_a_im, ssm_log_dt, ssm_b_re, ssm_b_im,
                                  ssm_c_re, ssm_c_im)
    perm = _scan_permutation()
    ssm = _ssm(u.reshape(bsz, seq, SSM_WIDTH), jnp.asarray(perm, BF16), jnp.asarray(perm.T, BF16),
               wb, wc, a, apow, row(ssm_d), w_glu, row(b_glu))

    x1, h2 = _merge(attn.reshape(t, ATTN_WIDTH), ssm.reshape(t, SSM_WIDTH), ga, gs, x2,
                    w_branch_attn, w_branch_ssm, w_out, row(ffn_norm_g), tm=MERGE_ROWS)

    out = _ffn(h2, x1, w_up, conv_w, row(conv_b), w_down, row(out_norm_g),
               seq=seq, tm=FFN_ROWS, tf=FFN_DFF_TILE)
    return out.reshape(bsz, seq, D_MODEL)


def kernel(x, attn_norm_g, w_in, b_in, attn_sinks, ssm_a_re, ssm_a_im, ssm_log_dt, ssm_b_re,
           ssm_b_im, ssm_c_re, ssm_c_im, ssm_d, w_glu, b_glu, w_branch_attn, w_branch_ssm, w_out,
           ffn_norm_g, w_up, conv_w, conv_b, w_down, final_norm_g):
    depth = w_in.shape[0]
    assert depth == 1, "the fused final norm assumes a single layer"
    return _layer(0, x, attn_norm_g[0], w_in, b_in[0], attn_sinks[0], ssm_a_re[0], ssm_a_im[0],
                  ssm_log_dt[0], ssm_b_re[0], ssm_b_im[0], ssm_c_re[0], ssm_c_im[0], ssm_d[0],
                  b_glu[0], ffn_norm_g[0], conv_w[0], conv_b[0], final_norm_g,
                  [w_glu, w_branch_attn, w_branch_ssm, w_out, w_up, w_down])
```

```python
import functools
import math

import jax
import jax.numpy as jnp
import numpy as np
from jax import lax
from jax.experimental import pallas as pl
from jax.experimental.pallas import tpu as pltpu

D_MODEL = 2048
N_Q_HEADS = 16
N_KV_HEADS = 2
HEAD_DIM = 64
Q_PER_KV = N_Q_HEADS // N_KV_HEADS
WINDOW = 128
BLOCK = 128
ATTN_WIDTH = N_Q_HEADS * HEAD_DIM
KV_WIDTH = N_KV_HEADS * HEAD_DIM
SSM_GROUP = 16
SSM_GROUPS = 32
SSM_WIDTH = SSM_GROUP * SSM_GROUPS
SSM_STATE = 64
D_FF = 5632
CONV_WIDTH = 3
RMS_EPS = 1e-6
IN_COLS = ATTN_WIDTH + 2 * KV_WIDTH + SSM_WIDTH + 2 * D_MODEL
NEG_BIG = -1e30

V7X_LANES = 128
V7X_SUBLANES = 8
V7X_VMEM_BYTES = 64 * 1024 * 1024
MIB = 1024 * 1024

BF16 = jnp.bfloat16
F32 = jnp.float32

IN_PROJ_ROWS = 256
MERGE_ROWS = 256
FFN_ROWS = 1024
FFN_DFF_TILE = 512
IN_PROJ_VMEM = 58 * MIB
ATTN_VMEM = 32 * MIB
SSM_VMEM = 48 * MIB
MERGE_VMEM = 56 * MIB
FFN_VMEM = 60 * MIB
assert max(IN_PROJ_VMEM, ATTN_VMEM, SSM_VMEM, MERGE_VMEM, FFN_VMEM) < V7X_VMEM_BYTES

_Q0 = 0
_K0 = ATTN_WIDTH
_V0 = _K0 + KV_WIDTH
_U0 = _V0 + KV_WIDTH
_GA0 = _U0 + SSM_WIDTH
_GS0 = _GA0 + D_MODEL

SCAN_CHUNKS = V7X_SUBLANES
SCAN_LEN = 64
SCAN_ROWS = SCAN_CHUNKS * SCAN_LEN
SSM_LANE_GROUPS = 4
LG_STATES = SSM_GROUPS * SSM_STATE // SSM_LANE_GROUPS
LG_CHANNELS = SSM_WIDTH // SSM_LANE_GROUPS


def _dot(a, b):
    return jnp.dot(a, b, preferred_element_type=F32)


def _dot_nt(a, b):
    return lax.dot_general(a, b, (((1,), (1,)), ((), ())), preferred_element_type=F32)


def _rms_norm(x, g):
    return x * lax.rsqrt(jnp.mean(x * x, axis=-1, keepdims=True) + RMS_EPS) * g


def _gelu(x):
    return 0.5 * x * (1.0 + lax.erf(x * np.float32(math.sqrt(0.5))))


def _const_spec(shape):
    nd = len(shape)
    return pl.BlockSpec(shape, lambda *_: (0,) * nd, pipeline_mode=pl.Buffered(1))


N_PROJ_OUT = 5
N_PROJ_SCRATCH = 3
W_IN_STAGE_ROWS = 128


def _in_proj_kernel(x_ref, g_ref, w_hbm, b_ref, *refs, layer):
    n_cast = (len(refs) - N_PROJ_OUT - N_PROJ_SCRATCH) // 2
    cast_in = refs[:n_cast]
    q_ref, kv_ref, u_ref, ga_ref, gs_ref = refs[n_cast:n_cast + N_PROJ_OUT]
    cast_out = refs[n_cast + N_PROJ_OUT:2 * n_cast + N_PROJ_OUT]
    w_scr, stage, sems = refs[2 * n_cast + N_PROJ_OUT:]

    @pl.when(pl.program_id(0) == 0)
    def _():
        n_slices = D_MODEL // W_IN_STAGE_ROWS

        def fetch(c):
            rows = pl.ds(c * W_IN_STAGE_ROWS, W_IN_STAGE_ROWS)
            return pltpu.make_async_copy(w_hbm.at[layer, rows, :], stage.at[c % 2], sems.at[c % 2])

        fetch(0).start()
        for c in range(n_slices):
            if c + 1 < n_slices:
                fetch(c + 1).start()
            fetch(c).wait()
            w_scr[c * W_IN_STAGE_ROWS:(c + 1) * W_IN_STAGE_ROWS, :] = stage[c % 2].astype(BF16)

    x = x_ref[...]
    scale = lax.rsqrt(jnp.mean(x * x, axis=-1, keepdims=True) + RMS_EPS)
    proj = _dot((x * g_ref[...]).astype(BF16), w_scr[...]) * scale + b_ref[...]
    q_ref[...] = (proj[:, _Q0:_K0] * (HEAD_DIM ** -0.5)).astype(BF16)
    kv_ref[...] = proj[:, _K0:_U0].astype(BF16)
    u_ref[...] = proj[:, _U0:_GA0]
    ga_ref[...] = proj[:, _GA0:_GS0]
    gs_ref[...] = proj[:, _GS0:]
    for src, dst in zip(cast_in, cast_out):
        dst[...] = src[...].astype(BF16)


def _cast_tiling(n_rows, n_steps):
    bf16_rows = 2 * V7X_SUBLANES
    for rows in range(bf16_rows, n_rows + 1, bf16_rows):
        if n_rows % rows == 0 and n_rows // rows <= n_steps and n_steps % (n_rows // rows) == 0:
            return rows, n_steps // (n_rows // rows)
    raise ValueError(f"no cast tiling for {n_rows} rows over {n_steps} steps")


def _in_proj(x2, g, w, b, cast_weights, layer, tm):
    t = x2.shape[0]
    n_steps = t // tm
    row = lambda width: pl.BlockSpec((tm, width), lambda i: (i, 0))
    cast_in_specs, cast_out_specs, cast_shapes = [], [], []
    for cw in cast_weights:
        _, n_rows, n_cols = cw.shape
        rows, hold = _cast_tiling(n_rows, n_steps)
        cast_in_specs.append(pl.BlockSpec((None, rows, n_cols),
                                          lambda i, hold=hold: (layer, i // hold, 0)))
        cast_out_specs.append(pl.BlockSpec((rows, n_cols), lambda i, hold=hold: (i // hold, 0)))
        cast_shapes.append(jax.ShapeDtypeStruct((n_rows, n_cols), BF16))
    outs = pl.pallas_call(
        functools.partial(_in_proj_kernel, layer=layer),
        grid=(n_steps,),
        in_specs=[row(D_MODEL), _const_spec((1, D_MODEL)),
                  pl.BlockSpec(memory_space=pl.ANY), _const_spec((1, IN_COLS))] + cast_in_specs,
        out_specs=[row(ATTN_WIDTH), row(2 * KV_WIDTH), row(SSM_WIDTH),
                   row(D_MODEL), row(D_MODEL)] + cast_out_specs,
        out_shape=[jax.ShapeDtypeStruct((t, ATTN_WIDTH), BF16),
                   jax.ShapeDtypeStruct((t, 2 * KV_WIDTH), BF16),
                   jax.ShapeDtypeStruct((t, SSM_WIDTH), F32),
                   jax.ShapeDtypeStruct((t, D_MODEL), F32),
                   jax.ShapeDtypeStruct((t, D_MODEL), F32)] + cast_shapes,
        scratch_shapes=[pltpu.VMEM((D_MODEL, IN_COLS), BF16),
                        pltpu.VMEM((2, W_IN_STAGE_ROWS, IN_COLS), F32),
                        pltpu.SemaphoreType.DMA((2,))],
        compiler_params=pltpu.CompilerParams(
            dimension_semantics=("arbitrary",), vmem_limit_bytes=IN_PROJ_VMEM),
        name="in_proj",
    )(x2, g, w, b, *cast_weights)
    return outs[:N_PROJ_OUT], outs[N_PROJ_OUT:]


ATTN_TQ = 1024
ATTN_SUB = ATTN_TQ // BLOCK
ATTN_GROUP = 8


def _attn_bias_table(sinks):
    q_idx = np.arange(BLOCK)[:, None]
    s_idx = np.arange(2 * BLOCK)[None, :]
    dist = q_idx + BLOCK - s_idx
    band = (dist >= 0) & (dist < WINDOW)
    assert not band[:, 0].any()
    slopes = 2.0 ** (-8.0 * np.arange(1, N_Q_HEADS + 1, dtype=np.float32) / N_Q_HEADS)
    alibi = -slopes[:, None, None].astype(np.float32) * dist.astype(np.float32)[None]
    general = np.where(band[None], alibi, np.float32(NEG_BIG))
    first = np.where((band & (s_idx >= BLOCK))[None], alibi, np.float32(NEG_BIG))
    table = jnp.asarray(np.stack([general, first]).astype(np.float32))
    sink_col = jnp.asarray((s_idx == 0)[None, None])
    return jnp.where(sink_col, sinks.astype(F32)[None, :, None, None], table)


ATTN_PATCH = 16


def _attn_kernel(q_ref, kvc_ref, kvp_ref, bias_ref, o_ref, ops_scr):
    n = pl.program_id(1)
    lane = lax.broadcasted_iota(jnp.int32, (1, 2 * HEAD_DIM), 1)
    lo = lane < HEAD_DIM

    def operands(kv):
        kv = kv.astype(F32)
        k_nat = kv[:, :KV_WIDTH]
        v_nat = kv[:, KV_WIDTH:]
        k_swp = pltpu.roll(k_nat, HEAD_DIM, 1)
        v_swp = pltpu.roll(v_nat, HEAD_DIM, 1)
        return (jnp.where(lo, k_nat, 0.0), jnp.where(lo, k_swp, 0.0),
                jnp.where(lo, 0.0, k_swp), jnp.where(lo, 0.0, k_nat),
                jnp.where(lo, v_nat, 1.0), jnp.where(lo, v_swp, 1.0),
                jnp.where(lo, 1.0, v_swp), jnp.where(lo, 1.0, v_nat))

    for idx, op in enumerate(operands(kvp_ref[...])):
        ops_scr[idx, 0:BLOCK, :] = op.astype(BF16)
    for idx, op in enumerate(operands(kvc_ref[...])):
        ops_scr[idx, BLOCK:, :] = op.astype(BF16)

    sink_row = lax.broadcasted_iota(jnp.int32, (ATTN_PATCH, 1), 0) == 0
    zero = jnp.zeros((1, 2 * HEAD_DIM), F32)
    sink_fill = (zero, zero, zero, zero,
                 jnp.where(lo, 0.0, 1.0), jnp.where(lo, 0.0, 1.0),
                 jnp.where(lo, 1.0, 0.0), jnp.where(lo, 1.0, 0.0))

    def window(idx, r0):
        top = ops_scr[idx, pl.ds(r0, ATTN_PATCH), :].astype(F32)
        top = jnp.where(sink_row, sink_fill[idx], top).astype(BF16)
        rest = ops_scr[idx, pl.ds(r0 + ATTN_PATCH, 2 * BLOCK - ATTN_PATCH), :]
        return jnp.concatenate([top, rest], axis=0)

    pairs_per_group = Q_PER_KV // 2
    group_pairs = [range(grp * pairs_per_group, (grp + 1) * pairs_per_group)
                   for grp in range(N_KV_HEADS)]

    def sub_blocks(it, carry):
        r0s = [pl.multiple_of((it * ATTN_GROUP + k) * BLOCK, BLOCK) for k in range(ATTN_GROUP)]
        wins, scores = {}, {}
        for k, r0 in enumerate(r0s):
            first = jnp.where(jnp.logical_and(n == 0, it * ATTN_GROUP + k == 0), 1, 0)
            wins[k] = [window(idx, r0) for idx in range(8)]
            q_all = q_ref[pl.ds(r0, BLOCK), :]
            for grp, pairs in enumerate(group_pairs):
                q_stack = jnp.concatenate(
                    [q_all[:, pr * 2 * HEAD_DIM:(pr + 1) * 2 * HEAD_DIM] for pr in pairs], axis=0)
                for odd in range(2):
                    bias = jnp.concatenate(
                        [bias_ref[first, 2 * pr + odd] for pr in pairs], axis=0)
                    scores[k, grp, odd] = _dot_nt(q_stack, wins[k][2 * odd + grp]) + bias
        probs = {}
        for key, s in scores.items():
            m = jnp.max(s, axis=-1, keepdims=True)
            probs[key] = jnp.exp(s - m).astype(BF16)
        res = {(k, grp, odd): _dot(p, wins[k][4 + 2 * odd + grp])
               for (k, grp, odd), p in probs.items()}
        for k, r0 in enumerate(r0s):
            for grp, pairs in enumerate(group_pairs):
                num = jnp.where(lo, res[k, grp, 0], res[k, grp, 1])
                den = pltpu.roll(jnp.where(lo, res[k, grp, 1], res[k, grp, 0]), HEAD_DIM, 1)
                out = (num / den).astype(BF16)
                for j, pr in enumerate(pairs):
                    o_ref[pl.ds(r0, BLOCK), pr * 2 * HEAD_DIM:(pr + 1) * 2 * HEAD_DIM] = (
                        out[j * BLOCK:(j + 1) * BLOCK, :])
        return carry

    lax.fori_loop(0, ATTN_SUB // ATTN_GROUP, sub_blocks, 0)


def _attention(q, kv, bias):
    bsz, seq, _ = q.shape
    nq = seq // ATTN_TQ
    return pl.pallas_call(
        _attn_kernel,
        grid=(bsz, nq),
        in_specs=[
            pl.BlockSpec((None, ATTN_TQ, ATTN_WIDTH), lambda b, n: (b, n, 0)),
            pl.BlockSpec((None, ATTN_TQ, 2 * KV_WIDTH), lambda b, n: (b, n, 0)),
            pl.BlockSpec((None, BLOCK, 2 * KV_WIDTH),
                         lambda b, n: (b, jnp.maximum(n * ATTN_SUB - 1, 0), 0)),
            _const_spec((2, N_Q_HEADS, BLOCK, 2 * BLOCK)),
        ],
        out_specs=pl.BlockSpec((None, ATTN_TQ, ATTN_WIDTH), lambda b, n: (b, n, 0)),
        out_shape=jax.ShapeDtypeStruct((bsz, seq, ATTN_WIDTH), BF16),
        scratch_shapes=[pltpu.VMEM((8, ATTN_TQ + BLOCK, 2 * HEAD_DIM), BF16)],
        compiler_params=pltpu.CompilerParams(
            dimension_semantics=("arbitrary", "arbitrary"), vmem_limit_bytes=ATTN_VMEM),
        name="attn",
    )(q, kv, kv, bias)


def _scan_permutation():
    p = np.zeros((SCAN_ROWS, SCAN_ROWS), np.float32)
    for c in range(SCAN_CHUNKS):
        for s in range(SCAN_LEN):
            p[s * SCAN_CHUNKS + c, c * SCAN_LEN + s] = 1.0
    return p


def _ssm_kernel(u_ref, perm_ref, permt_ref, wb_ref, wc_ref, a_ref, apow_ref, d_ref,
                wglu_ref, bglu_ref, o_ref, carry_scr, bu_scr, xs_scr, x0_scr, y_scr, *, n_seq):
    @pl.when(pl.program_id(0) == 0)
    def _():
        carry_scr[...] = jnp.zeros_like(carry_scr)

    perm = perm_ref[...]
    u_perm, ub = {}, {}

    def reorder(b):
        u = u_ref[b]
        u_hi = u.astype(BF16)
        u_lo = (u - u_hi.astype(F32)).astype(BF16)
        up_hi = _dot(perm, u_hi)
        u_perm[b] = up_hi + _dot(perm, u_lo)
        ub[b] = up_hi.astype(BF16)

    def project_in(b, lg):
        bu_scr[b, lg] = _dot(ub[b][:, lg * LG_CHANNELS:(lg + 1) * LG_CHANNELS], wb_ref[lg])

    def recurrence(b, lg, a_re, a_im, state, store):
        xr, xi = state
        for s in range(0, SCAN_LEN, 2):
            pair_re, pair_im = [], []
            for t in (s, s + 1):
                rows = slice(t * SCAN_CHUNKS, (t + 1) * SCAN_CHUNKS)
                br = bu_scr[b, lg, rows, :LG_STATES]
                bi = bu_scr[b, lg, rows, LG_STATES:]
                xr, xi = a_re * xr - a_im * xi + br, a_re * xi + a_im * xr + bi
                pair_re.append(xr)
                pair_im.append(xi)
            if store:
                rows = slice(s * SCAN_CHUNKS, (s + 2) * SCAN_CHUNKS)
                xs_scr[b, lg, rows, :LG_STATES] = jnp.concatenate(pair_re, axis=0).astype(BF16)
                xs_scr[b, lg, rows, LG_STATES:] = jnp.concatenate(pair_im, axis=0).astype(BF16)
        return xr, xi

    def decay(lg):
        return (jnp.broadcast_to(a_ref[lg, 0:1, :], (SCAN_CHUNKS, LG_STATES)),
                jnp.broadcast_to(a_ref[lg, 1:2, :], (SCAN_CHUNKS, LG_STATES)))

    def scan_local(b, lg):
        a_re, a_im = decay(lg)
        zeros = jnp.zeros((SCAN_CHUNKS, LG_STATES), F32)
        return recurrence(b, lg, a_re, a_im, (zeros, zeros), False)

    def scan_true(b, lg, ends):
        a_re, a_im = decay(lg)
        e_re, e_im = ends
        p_re = apow_ref[lg, 0:1, :]
        p_im = apow_ref[lg, 1:2, :]
        cur_re = carry_scr[b, lg, 0:1, :]
        cur_im = carry_scr[b, lg, 1:2, :]
        for c in range(SCAN_CHUNKS):
            x0_scr[b, lg, 0, c:c + 1, :] = cur_re
            x0_scr[b, lg, 1, c:c + 1, :] = cur_im
            nxt_re = p_re * cur_re - p_im * cur_im + e_re[c:c + 1, :]
            nxt_im = p_re * cur_im + p_im * cur_re + e_im[c:c + 1, :]
            cur_re, cur_im = nxt_re, nxt_im
        carry_scr[b, lg, 0:1, :] = cur_re
        carry_scr[b, lg, 1:2, :] = cur_im
        recurrence(b, lg, a_re, a_im, (x0_scr[b, lg, 0], x0_scr[b, lg, 1]), True)

    def project_out(b, lg):
        y_scr[b, :, lg * LG_CHANNELS:(lg + 1) * LG_CHANNELS] = _dot(xs_scr[b, lg], wc_ref[lg])

    def finish(b):
        y = y_scr[b] + d_ref[...] * u_perm[b]
        z = _dot(_gelu(y).astype(BF16), wglu_ref[...]) + bglu_ref[...]
        ssm = z[:, :SSM_WIDTH] * jax.nn.sigmoid(z[:, SSM_WIDTH:])
        o_ref[b] = _dot(permt_ref[...], ssm.astype(BF16)).astype(BF16)

    chains = [(b, lg) for lg in range(SSM_LANE_GROUPS) for b in range(n_seq)]
    for b in range(n_seq):
        reorder(b)
    for c in chains:
        project_in(*c)
    ends = {c: scan_local(*c) for c in chains}
    for c in chains:
        scan_true(*c, ends[c])
    for c in chains:
        project_out(*c)
    for b in range(n_seq):
        finish(b)


def _ssm(u, perm, permt, wb, wc, a, apow, d, wglu, bglu):
    bsz, seq, _ = u.shape
    per_seq = lambda *shape, dt=F32: pltpu.VMEM((bsz,) + shape, dt)
    return pl.pallas_call(
        functools.partial(_ssm_kernel, n_seq=bsz),
        grid=(seq // SCAN_ROWS,),
        in_specs=[
            pl.BlockSpec((bsz, SCAN_ROWS, SSM_WIDTH), lambda s: (0, s, 0)),
            _const_spec((SCAN_ROWS, SCAN_ROWS)),
            _const_spec((SCAN_ROWS, SCAN_ROWS)),
            _const_spec((SSM_LANE_GROUPS, LG_CHANNELS, 2 * LG_STATES)),
            _const_spec((SSM_LANE_GROUPS, 2 * LG_STATES, LG_CHANNELS)),
            _const_spec((SSM_LANE_GROUPS, 2, LG_STATES)),
            _const_spec((SSM_LANE_GROUPS, 2, LG_STATES)),
            _const_spec((1, SSM_WIDTH)),
            _const_spec((SSM_WIDTH, 2 * SSM_WIDTH)),
            _const_spec((1, 2 * SSM_WIDTH)),
        ],
        out_specs=pl.BlockSpec((bsz, SCAN_ROWS, SSM_WIDTH), lambda s: (0, s, 0)),
        out_shape=jax.ShapeDtypeStruct((bsz, seq, SSM_WIDTH), BF16),
        scratch_shapes=[
            per_seq(SSM_LANE_GROUPS, 2, LG_STATES),
            per_seq(SSM_LANE_GROUPS, SCAN_ROWS, 2 * LG_STATES),
            per_seq(SSM_LANE_GROUPS, SCAN_ROWS, 2 * LG_STATES, dt=BF16),
            per_seq(SSM_LANE_GROUPS, 2, SCAN_CHUNKS, LG_STATES),
            per_seq(SCAN_ROWS, SSM_WIDTH),
        ],
        compiler_params=pltpu.CompilerParams(
            dimension_semantics=("arbitrary",), vmem_limit_bytes=SSM_VMEM),
        name="ssm",
    )(u, perm, permt, wb, wc, a, apow, d, wglu, bglu)


def _ssm_params(a_re, a_im, log_dt, b_re, b_im, c_re, c_im):
    dt = jnp.exp(log_dt)[:, None]
    mag = jnp.exp(a_re * dt)
    ab_re = mag * jnp.cos(a_im * dt)
    ab_im = mag * jnp.sin(a_im * dt)
    nr = ab_re - 1.0
    ni = ab_im
    den = a_re * a_re + a_im * a_im
    z_re = (nr * a_re + ni * a_im) / den
    z_im = (ni * a_re - nr * a_im) / den
    bb_re = z_re[..., None] * b_re - z_im[..., None] * b_im
    bb_im = z_re[..., None] * b_im + z_im[..., None] * b_re

    gl = SSM_GROUPS // SSM_LANE_GROUPS
    eye = jnp.eye(gl, dtype=F32)

    def in_mat(bb):
        bb = bb.reshape(SSM_LANE_GROUPS, gl, SSM_STATE, SSM_GROUP)
        m = jnp.einsum('lgph,gk->lghkp', bb, eye)
        return m.reshape(SSM_LANE_GROUPS, LG_CHANNELS, LG_STATES)

    def out_mat(cc):
        cc = cc.reshape(SSM_LANE_GROUPS, gl, SSM_GROUP, SSM_STATE)
        m = jnp.einsum('lghp,gk->lgpkh', cc, eye)
        return m.reshape(SSM_LANE_GROUPS, LG_STATES, LG_CHANNELS)

    wb = jnp.concatenate([in_mat(bb_re), in_mat(bb_im)], axis=-1).astype(BF16)
    wc = jnp.concatenate([out_mat(c_re), out_mat(-c_im)], axis=-2).astype(BF16)

    pr, pi = ab_re, ab_im
    for _ in range(int(math.log2(SCAN_LEN))):
        pr, pi = pr * pr - pi * pi, 2.0 * pr * pi
    lanes = lambda v: v.reshape(SSM_LANE_GROUPS, LG_STATES)
    a = jnp.stack([lanes(ab_re), lanes(ab_im)], axis=1)
    apow = jnp.stack([lanes(pr), lanes(pi)], axis=1)
    return wb, wc, a, apow


def _merge_kernel(attn_ref, ssm_ref, ga_ref, gs_ref, x_ref, wba_ref, wbs_ref, wout_ref,
                  g_ref, x1_ref, h2_ref):
    a = _dot(attn_ref[...], wba_ref[...])
    s = _dot(ssm_ref[...], wbs_ref[...])
    merged = jax.nn.sigmoid(ga_ref[...]) * a + jax.nn.sigmoid(gs_ref[...]) * s
    x1 = x_ref[...] + _dot(merged.astype(BF16), wout_ref[...])
    x1_ref[...] = x1
    h2_ref[...] = _rms_norm(x1, g_ref[...]).astype(BF16)


def _merge(attn, ssm, ga, gs, x2, wba, wbs, wout, g, tm):
    t = x2.shape[0]
    row = lambda width: pl.BlockSpec((tm, width), lambda i: (i, 0))
    return pl.pallas_call(
        _merge_kernel,
        grid=(t // tm,),
        in_specs=[row(ATTN_WIDTH), row(SSM_WIDTH), row(D_MODEL), row(D_MODEL), row(D_MODEL),
                  _const_spec((ATTN_WIDTH, D_MODEL)), _const_spec((SSM_WIDTH, D_MODEL)),
                  _const_spec((D_MODEL, D_MODEL)), _const_spec((1, D_MODEL))],
        out_specs=[row(D_MODEL), row(D_MODEL)],
        out_shape=[jax.ShapeDtypeStruct((t, D_MODEL), F32),
                   jax.ShapeDtypeStruct((t, D_MODEL), BF16)],
        compiler_params=pltpu.CompilerParams(
            dimension_semantics=("arbitrary",), vmem_limit_bytes=MERGE_VMEM),
        name="merge",
    )(attn, ssm, ga, gs, x2, wba, wbs, wout, g)


FFN_TAIL = V7X_SUBLANES
FFN_CHUNK = 256


def _ffn_kernel(h_ref, x1_hbm, wv_ref, wg_ref, cw_ref, cb_ref, wd_ref, gf_ref,
                o_ref, x1_scr, act_scr, tail_scr, x1_sem, *, tm, tf, seq):
    i = pl.program_id(0)
    j = pl.program_id(1)
    last = pl.num_programs(1) - 1

    def x1_copy():
        return pltpu.make_async_copy(x1_hbm.at[pl.ds(i * tm, tm), :], x1_scr, x1_sem)

    @pl.when(jnp.logical_and(i == 0, j == 0))
    def _():
        tail_scr[...] = jnp.zeros_like(tail_scr)

    @pl.when(j == 0)
    def _():
        x1_copy().start()
        o_ref[...] = jnp.zeros_like(o_ref)

    seq_start = (i * tm) % seq == 0
    row = lax.broadcasted_iota(jnp.int32, (FFN_TAIL, FFN_CHUNK), 0)
    h = h_ref[...]
    for c in range(tf // FFN_CHUNK):
        cols = slice(c * FFN_CHUNK, (c + 1) * FFN_CHUNK)
        val = _dot(h, wv_ref[:, cols])
        gate = _dot(h, wg_ref[:, cols])
        prev = jnp.where(seq_start, 0.0, tail_scr[j, :, cols])
        tail_scr[j, :, cols] = gate[tm - FFN_TAIL:, :]
        conv = cb_ref[:, cols] + cw_ref[CONV_WIDTH - 1:CONV_WIDTH, cols] * gate
        for back in range(1, CONV_WIDTH):
            shifted = pltpu.roll(gate, back, 0)
            top = jnp.where(row < back, pltpu.roll(prev, back, 0), shifted[:FFN_TAIL])
            shifted = jnp.concatenate([top, shifted[FFN_TAIL:]], axis=0)
            k = CONV_WIDTH - 1 - back
            conv = conv + cw_ref[k:k + 1, cols] * shifted
        act_scr[:, cols] = (val * _gelu(conv)).astype(BF16)
    o_ref[...] += _dot(act_scr[...], wd_ref[...])

    @pl.when(j == last)
    def _():
        x1_copy().wait()
        o_ref[...] = _rms_norm(x1_scr[...] + o_ref[...], gf_ref[...])


def _ffn(h2, x1, w_up, conv_w, conv_b, w_down, gf, seq, tm, tf):
    t = h2.shape[0]
    nf = D_FF // tf
    kern = functools.partial(_ffn_kernel, tm=tm, tf=tf, seq=seq)
    return pl.pallas_call(
        kern,
        grid=(t // tm, nf),
        in_specs=[
            pl.BlockSpec((tm, D_MODEL), lambda i, j: (i, 0)),
            pl.BlockSpec(memory_space=pl.ANY),
            pl.BlockSpec((D_MODEL, tf), lambda i, j: (0, j)),
            pl.BlockSpec((D_MODEL, tf), lambda i, j: (0, j + nf)),
            pl.BlockSpec((CONV_WIDTH, tf), lambda i, j: (0, j)),
            pl.BlockSpec((1, tf), lambda i, j: (0, j)),
            pl.BlockSpec((tf, D_MODEL), lambda i, j: (j, 0)),
            _const_spec((1, D_MODEL)),
        ],
        out_specs=pl.BlockSpec((tm, D_MODEL), lambda i, j: (i, 0)),
        out_shape=jax.ShapeDtypeStruct((t, D_MODEL), F32),
        scratch_shapes=[pltpu.VMEM((tm, D_MODEL), F32),
                        pltpu.VMEM((tm, tf), BF16),
                        pltpu.VMEM((nf, FFN_TAIL, tf), F32),
                        pltpu.SemaphoreType.DMA(())],
        compiler_params=pltpu.CompilerParams(
            dimension_semantics=("arbitrary", "arbitrary"), vmem_limit_bytes=FFN_VMEM),
        name="ffn",
    )(h2, x1, w_up, w_up, conv_w, conv_b, w_down, gf)


def _layer(layer, x, attn_norm_g, w_in, b_in, attn_sinks, ssm_a_re, ssm_a_im, ssm_log_dt,
           ssm_b_re, ssm_b_im, ssm_c_re, ssm_c_im, ssm_d, b_glu, ffn_norm_g, conv_w, conv_b,
           out_norm_g, stacked_weights):
    bsz, seq, width = x.shape
    t = bsz * seq
    assert width == D_MODEL and x.dtype == F32
    assert seq % ATTN_TQ == 0 and seq % SCAN_ROWS == 0 and seq % FFN_ROWS == 0
    assert t % IN_PROJ_ROWS == 0 and t % MERGE_ROWS == 0
    assert D_FF % FFN_DFF_TILE == 0 and FFN_DFF_TILE % FFN_CHUNK == 0
    x2 = x.reshape(t, D_MODEL)
    row = lambda v: v.reshape(1, -1)

    (q, kv, u, ga, gs), (w_glu, w_branch_attn, w_branch_ssm, w_out, w_up, w_down) = _in_proj(
        x2, row(attn_norm_g), w_in, row(b_in), stacked_weights, layer, tm=IN_PROJ_ROWS)

    attn = _attention(q.reshape(bsz, seq, ATTN_WIDTH), kv.reshape(bsz, seq, 2 * KV_WIDTH),
                      _attn_bias_table(attn_sinks))

    wb, wc, a, apow = _ssm_params(ssm_a_re, ssm_a_im, ssm_log_dt, ssm_b_re, ssm_b_im,
                                  ssm_c_re, ssm_c_im)
    perm = _scan_permutation()
    ssm = _ssm(u.reshape(bsz, seq, SSM_WIDTH), jnp.asarray(perm, BF16), jnp.asarray(perm.T, BF16),
               wb, wc, a, apow, row(ssm_d), w_glu, row(b_glu))

    x1, h2 = _merge(attn.reshape(t, ATTN_WIDTH), ssm.reshape(t, SSM_WIDTH), ga, gs, x2,
                    w_branch_attn, w_branch_ssm, w_out, row(ffn_norm_g), tm=MERGE_ROWS)

    out = _ffn(h2, x1, w_up, conv_w, row(conv_b), w_down, row(out_norm_g),
               seq=seq, tm=FFN_ROWS, tf=FFN_DFF_TILE)
    return out.reshape(bsz, seq, D_MODEL)


def kernel(x, attn_norm_g, w_in, b_in, attn_sinks, ssm_a_re, ssm_a_im, ssm_log_dt, ssm_b_re,
           ssm_b_im, ssm_c_re, ssm_c_im, ssm_d, w_glu, b_glu, w_branch_attn, w_branch_ssm, w_out,
           ffn_norm_g, w_up, conv_w, conv_b, w_down, final_norm_g):
    depth = w_in.shape[0]
    assert depth == 1, "the fused final norm assumes a single layer"
    return _layer(0, x, attn_norm_g[0], w_in, b_in[0], attn_sinks[0], ssm_a_re[0], ssm_a_im[0],
                  ssm_log_dt[0], ssm_b_re[0], ssm_b_im[0], ssm_c_re[0], ssm_c_im[0], ssm_d[0],
                  b_glu[0], ffn_norm_g[0], conv_w[0], conv_b[0], final_norm_g,
                  [w_glu, w_branch_attn, w_branch_ssm, w_out, w_up, w_down])
```

```python
import functools
import math

import jax
import jax.numpy as jnp
import numpy as np
from jax import lax
from jax.experimental import pallas as pl
from jax.experimental.pallas import tpu as pltpu

D_MODEL = 2048
N_Q_HEADS = 16
N_KV_HEADS = 2
HEAD_DIM = 64
Q_PER_KV = N_Q_HEADS // N_KV_HEADS
WINDOW = 128
BLOCK = 128
ATTN_WIDTH = N_Q_HEADS * HEAD_DIM
KV_WIDTH = N_KV_HEADS * HEAD_DIM
SSM_GROUP = 16
SSM_GROUPS = 32
SSM_WIDTH = SSM_GROUP * SSM_GROUPS
SSM_STATE = 64
D_FF = 5632
CONV_WIDTH = 3
RMS_EPS = 1e-6
IN_COLS = ATTN_WIDTH + 2 * KV_WIDTH + SSM_WIDTH + 2 * D_MODEL
NEG_BIG = -1e30

V7X_LANES = 128
V7X_SUBLANES = 8
V7X_VMEM_BYTES = 64 * 1024 * 1024
MIB = 1024 * 1024

BF16 = jnp.bfloat16
F32 = jnp.float32

IN_PROJ_ROWS = 256
MERGE_ROWS = 256
FFN_ROWS = 1024
FFN_DFF_TILE = 512
IN_PROJ_VMEM = 58 * MIB
ATTN_VMEM = 32 * MIB
SSM_VMEM = 48 * MIB
MERGE_VMEM = 56 * MIB
FFN_VMEM = 60 * MIB
assert max(IN_PROJ_VMEM, ATTN_VMEM, SSM_VMEM, MERGE_VMEM, FFN_VMEM) < V7X_VMEM_BYTES

_Q0 = 0
_K0 = ATTN_WIDTH
_V0 = _K0 + KV_WIDTH
_U0 = _V0 + KV_WIDTH
_GA0 = _U0 + SSM_WIDTH
_GS0 = _GA0 + D_MODEL

SCAN_CHUNKS = V7X_SUBLANES
SCAN_LEN = 64
SCAN_ROWS = SCAN_CHUNKS * SCAN_LEN
SSM_LANE_GROUPS = 4
LG_STATES = SSM_GROUPS * SSM_STATE // SSM_LANE_GROUPS
LG_CHANNELS = SSM_WIDTH // SSM_LANE_GROUPS


def _dot(a, b):
    return jnp.dot(a, b, preferred_element_type=F32)


def _dot_nt(a, b):
    return lax.dot_general(a, b, (((1,), (1,)), ((), ())), preferred_element_type=F32)


def _rms_norm(x, g):
    return x * lax.rsqrt(jnp.mean(x * x, axis=-1, keepdims=True) + RMS_EPS) * g


def _gelu(x):
    return 0.5 * x * (1.0 + lax.erf(x * np.float32(math.sqrt(0.5))))


def _const_spec(shape):
    nd = len(shape)
    return pl.BlockSpec(shape, lambda *_: (0,) * nd, pipeline_mode=pl.Buffered(1))


N_PROJ_OUT = 5
N_PROJ_SCRATCH = 3
W_IN_STAGE_ROWS = 128


def _in_proj_kernel(x_ref, g_ref, w_hbm, b_ref, *refs, layer):
    n_cast = (len(refs) - N_PROJ_OUT - N_PROJ_SCRATCH) // 2
    cast_in = refs[:n_cast]
    q_ref, kv_ref, u_ref, ga_ref, gs_ref = refs[n_cast:n_cast + N_PROJ_OUT]
    cast_out = refs[n_cast + N_PROJ_OUT:2 * n_cast + N_PROJ_OUT]
    w_scr, stage, sems = refs[2 * n_cast + N_PROJ_OUT:]

    @pl.when(pl.program_id(0) == 0)
    def _():
        n_slices = D_MODEL // W_IN_STAGE_ROWS

        def fetch(c):
            rows = pl.ds(c * W_IN_STAGE_ROWS, W_IN_STAGE_ROWS)
            return pltpu.make_async_copy(w_hbm.at[layer, rows, :], stage.at[c % 2], sems.at[c % 2])

        fetch(0).start()
        for c in range(n_slices):
            if c + 1 < n_slices:
                fetch(c + 1).start()
            fetch(c).wait()
            w_scr[c * W_IN_STAGE_ROWS:(c + 1) * W_IN_STAGE_ROWS, :] = stage[c % 2].astype(BF16)

    h = _rms_norm(x_ref[...], g_ref[...]).astype(BF16)
    proj = _dot(h, w_scr[...]) + b_ref[...]
    q_ref[...] = (proj[:, _Q0:_K0] * (HEAD_DIM ** -0.5)).astype(BF16)
    kv_ref[...] = proj[:, _K0:_U0].astype(BF16)
    u_ref[...] = proj[:, _U0:_GA0]
    ga_ref[...] = proj[:, _GA0:_GS0]
    gs_ref[...] = proj[:, _GS0:]
    for src, dst in zip(cast_in, cast_out):
        dst[...] = src[...].astype(BF16)


def _cast_tiling(n_rows, n_steps):
    bf16_rows = 2 * V7X_SUBLANES
    for rows in range(bf16_rows, n_rows + 1, bf16_rows):
        if n_rows % rows == 0 and n_rows // rows <= n_steps and n_steps % (n_rows // rows) == 0:
            return rows, n_steps // (n_rows // rows)
    raise ValueError(f"no cast tiling for {n_rows} rows over {n_steps} steps")


def _in_proj(x2, g, w, b, cast_weights, layer, tm):
    t = x2.shape[0]
    n_steps = t // tm
    row = lambda width: pl.BlockSpec((tm, width), lambda i: (i, 0))
    cast_in_specs, cast_out_specs, cast_shapes = [], [], []
    for cw in cast_weights:
        _, n_rows, n_cols = cw.shape
        rows, hold = _cast_tiling(n_rows, n_steps)
        cast_in_specs.append(pl.BlockSpec((None, rows, n_cols),
                                          lambda i, hold=hold: (layer, i // hold, 0)))
        cast_out_specs.append(pl.BlockSpec((rows, n_cols), lambda i, hold=hold: (i // hold, 0)))
        cast_shapes.append(jax.ShapeDtypeStruct((n_rows, n_cols), BF16))
    outs = pl.pallas_call(
        functools.partial(_in_proj_kernel, layer=layer),
        grid=(n_steps,),
        in_specs=[row(D_MODEL), _const_spec((1, D_MODEL)),
                  pl.BlockSpec(memory_space=pl.ANY), _const_spec((1, IN_COLS))] + cast_in_specs,
        out_specs=[row(ATTN_WIDTH), row(2 * KV_WIDTH), row(SSM_WIDTH),
                   row(D_MODEL), row(D_MODEL)] + cast_out_specs,
        out_shape=[jax.ShapeDtypeStruct((t, ATTN_WIDTH), BF16),
                   jax.ShapeDtypeStruct((t, 2 * KV_WIDTH), BF16),
                   jax.ShapeDtypeStruct((t, SSM_WIDTH), F32),
                   jax.ShapeDtypeStruct((t, D_MODEL), F32),
                   jax.ShapeDtypeStruct((t, D_MODEL), F32)] + cast_shapes,
        scratch_shapes=[pltpu.VMEM((D_MODEL, IN_COLS), BF16),
                        pltpu.VMEM((2, W_IN_STAGE_ROWS, IN_COLS), F32),
                        pltpu.SemaphoreType.DMA((2,))],
        compiler_params=pltpu.CompilerParams(
            dimension_semantics=("arbitrary",), vmem_limit_bytes=IN_PROJ_VMEM),
        name="in_proj",
    )(x2, g, w, b, *cast_weights)
    return outs[:N_PROJ_OUT], outs[N_PROJ_OUT:]


ATTN_TQ = 1024
ATTN_SUB = ATTN_TQ // BLOCK
ATTN_GROUP = 8


def _attn_bias_table(sinks):
    q_idx = np.arange(BLOCK)[:, None]
    s_idx = np.arange(2 * BLOCK)[None, :]
    dist = q_idx + BLOCK - s_idx
    band = (dist >= 0) & (dist < WINDOW)
    assert not band[:, 0].any()
    slopes = 2.0 ** (-8.0 * np.arange(1, N_Q_HEADS + 1, dtype=np.float32) / N_Q_HEADS)
    alibi = -slopes[:, None, None].astype(np.float32) * dist.astype(np.float32)[None]
    general = np.where(band[None], alibi, np.float32(NEG_BIG))
    first = np.where((band & (s_idx >= BLOCK))[None], alibi, np.float32(NEG_BIG))
    table = jnp.asarray(np.stack([general, first]).astype(np.float32))
    sink_col = jnp.asarray((s_idx == 0)[None, None])
    return jnp.where(sink_col, sinks.astype(F32)[None, :, None, None], table)


ATTN_PATCH = 16


def _attn_kernel(q_ref, kvc_ref, kvp_ref, bias_ref, o_ref, ops_scr):
    n = pl.program_id(1)
    lane = lax.broadcasted_iota(jnp.int32, (1, 2 * HEAD_DIM), 1)
    lo = lane < HEAD_DIM

    def operands(kv):
        kv = kv.astype(F32)
        k_nat = kv[:, :KV_WIDTH]
        v_nat = kv[:, KV_WIDTH:]
        k_swp = pltpu.roll(k_nat, HEAD_DIM, 1)
        v_swp = pltpu.roll(v_nat, HEAD_DIM, 1)
        return (jnp.where(lo, k_nat, 0.0), jnp.where(lo, k_swp, 0.0),
                jnp.where(lo, 0.0, k_swp), jnp.where(lo, 0.0, k_nat),
                jnp.where(lo, v_nat, 1.0), jnp.where(lo, v_swp, 1.0),
                jnp.where(lo, 1.0, v_swp), jnp.where(lo, 1.0, v_nat))

    for idx, op in enumerate(operands(kvp_ref[...])):
        ops_scr[idx, 0:BLOCK, :] = op.astype(BF16)
    for idx, op in enumerate(operands(kvc_ref[...])):
        ops_scr[idx, BLOCK:, :] = op.astype(BF16)

    sink_row = lax.broadcasted_iota(jnp.int32, (ATTN_PATCH, 1), 0) == 0
    zero = jnp.zeros((1, 2 * HEAD_DIM), F32)
    sink_fill = (zero, zero, zero, zero,
                 jnp.where(lo, 0.0, 1.0), jnp.where(lo, 0.0, 1.0),
                 jnp.where(lo, 1.0, 0.0), jnp.where(lo, 1.0, 0.0))

    def window(idx, r0):
        top = ops_scr[idx, pl.ds(r0, ATTN_PATCH), :].astype(F32)
        top = jnp.where(sink_row, sink_fill[idx], top).astype(BF16)
        rest = ops_scr[idx, pl.ds(r0 + ATTN_PATCH, 2 * BLOCK - ATTN_PATCH), :]
        return jnp.concatenate([top, rest], axis=0)

    pairs_per_group = Q_PER_KV // 2
    group_pairs = [range(grp * pairs_per_group, (grp + 1) * pairs_per_group)
                   for grp in range(N_KV_HEADS)]

    def sub_blocks(it, carry):
        r0s = [pl.multiple_of((it * ATTN_GROUP + k) * BLOCK, BLOCK) for k in range(ATTN_GROUP)]
        wins, scores = {}, {}
        for k, r0 in enumerate(r0s):
            first = jnp.where(jnp.logical_and(n == 0, it * ATTN_GROUP + k == 0), 1, 0)
            wins[k] = [window(idx, r0) for idx in range(8)]
            q_all = q_ref[pl.ds(r0, BLOCK), :]
            for grp, pairs in enumerate(group_pairs):
                q_stack = jnp.concatenate(
                    [q_all[:, pr * 2 * HEAD_DIM:(pr + 1) * 2 * HEAD_DIM] for pr in pairs], axis=0)
                for odd in range(2):
                    bias = jnp.concatenate(
                        [bias_ref[first, 2 * pr + odd] for pr in pairs], axis=0)
                    scores[k, grp, odd] = _dot_nt(q_stack, wins[k][2 * odd + grp]) + bias
        probs = {}
        for key, s in scores.items():
            m = jnp.max(s, axis=-1, keepdims=True)
            probs[key] = jnp.exp(s - m).astype(BF16)
        res = {(k, grp, odd): _dot(p, wins[k][4 + 2 * odd + grp])
               for (k, grp, odd), p in probs.items()}
        for k, r0 in enumerate(r0s):
            for grp, pairs in enumerate(group_pairs):
                num = jnp.where(lo, res[k, grp, 0], res[k, grp, 1])
                den = pltpu.roll(jnp.where(lo, res[k, grp, 1], res[k, grp, 0]), HEAD_DIM, 1)
                out = (num / den).astype(BF16)
                for j, pr in enumerate(pairs):
                    o_ref[pl.ds(r0, BLOCK), pr * 2 * HEAD_DIM:(pr + 1) * 2 * HEAD_DIM] = (
                        out[j * BLOCK:(j + 1) * BLOCK, :])
        return carry

    lax.fori_loop(0, ATTN_SUB // ATTN_GROUP, sub_blocks, 0)


def _attention(q, kv, bias):
    bsz, seq, _ = q.shape
    nq = seq // ATTN_TQ
    return pl.pallas_call(
        _attn_kernel,
        grid=(bsz, nq),
        in_specs=[
            pl.BlockSpec((None, ATTN_TQ, ATTN_WIDTH), lambda b, n: (b, n, 0)),
            pl.BlockSpec((None, ATTN_TQ, 2 * KV_WIDTH), lambda b, n: (b, n, 0)),
            pl.BlockSpec((None, BLOCK, 2 * KV_WIDTH),
                         lambda b, n: (b, jnp.maximum(n * ATTN_SUB - 1, 0), 0)),
            _const_spec((2, N_Q_HEADS, BLOCK, 2 * BLOCK)),
        ],
        out_specs=pl.BlockSpec((None, ATTN_TQ, ATTN_WIDTH), lambda b, n: (b, n, 0)),
        out_shape=jax.ShapeDtypeStruct((bsz, seq, ATTN_WIDTH), BF16),
        scratch_shapes=[pltpu.VMEM((8, ATTN_TQ + BLOCK, 2 * HEAD_DIM), BF16)],
        compiler_params=pltpu.CompilerParams(
            dimension_semantics=("arbitrary", "arbitrary"), vmem_limit_bytes=ATTN_VMEM),
        name="attn",
    )(q, kv, kv, bias)


def _scan_permutation():
    p = np.zeros((SCAN_ROWS, SCAN_ROWS), np.float32)
    for c in range(SCAN_CHUNKS):
        for s in range(SCAN_LEN):
            p[s * SCAN_CHUNKS + c, c * SCAN_LEN + s] = 1.0
    return p


def _ssm_kernel(u_ref, perm_ref, permt_ref, wb_ref, wc_ref, a_ref, apow_ref, d_ref,
                wglu_ref, bglu_ref, o_ref, carry_scr, bu_scr, xs_scr, x0_scr, y_scr, *, n_seq):
    @pl.when(pl.program_id(0) == 0)
    def _():
        carry_scr[...] = jnp.zeros_like(carry_scr)

    perm = perm_ref[...]
    u_perm, ub = {}, {}

    def reorder(b):
        u = u_ref[b]
        u_hi = u.astype(BF16)
        u_lo = (u - u_hi.astype(F32)).astype(BF16)
        up_hi = _dot(perm, u_hi)
        u_perm[b] = up_hi + _dot(perm, u_lo)
        ub[b] = up_hi.astype(BF16)

    def project_in(b, lg):
        bu_scr[b, lg] = _dot(ub[b][:, lg * LG_CHANNELS:(lg + 1) * LG_CHANNELS], wb_ref[lg])

    def recurrence(b, lg, a_re, a_im, state, store):
        xr, xi = state
        for s in range(0, SCAN_LEN, 2):
            pair_re, pair_im = [], []
            for t in (s, s + 1):
                rows = slice(t * SCAN_CHUNKS, (t + 1) * SCAN_CHUNKS)
                br = bu_scr[b, lg, rows, :LG_STATES]
                bi = bu_scr[b, lg, rows, LG_STATES:]
                xr, xi = a_re * xr - a_im * xi + br, a_re * xi + a_im * xr + bi
                pair_re.append(xr)
                pair_im.append(xi)
            if store:
                rows = slice(s * SCAN_CHUNKS, (s + 2) * SCAN_CHUNKS)
                xs_scr[b, lg, rows, :LG_STATES] = jnp.concatenate(pair_re, axis=0).astype(BF16)
                xs_scr[b, lg, rows, LG_STATES:] = jnp.concatenate(pair_im, axis=0).astype(BF16)
        return xr, xi

    def scan(b, lg):
        a_re = jnp.broadcast_to(a_ref[lg, 0:1, :], (SCAN_CHUNKS, LG_STATES))
        a_im = jnp.broadcast_to(a_ref[lg, 1:2, :], (SCAN_CHUNKS, LG_STATES))
        zeros = jnp.zeros((SCAN_CHUNKS, LG_STATES), F32)
        e_re, e_im = recurrence(b, lg, a_re, a_im, (zeros, zeros), False)
        p_re = apow_ref[lg, 0:1, :]
        p_im = apow_ref[lg, 1:2, :]
        cur_re = carry_scr[b, lg, 0:1, :]
        cur_im = carry_scr[b, lg, 1:2, :]
        for c in range(SCAN_CHUNKS):
            x0_scr[b, lg, 0, c:c + 1, :] = cur_re
            x0_scr[b, lg, 1, c:c + 1, :] = cur_im
            nxt_re = p_re * cur_re - p_im * cur_im + e_re[c:c + 1, :]
            nxt_im = p_re * cur_im + p_im * cur_re + e_im[c:c + 1, :]
            cur_re, cur_im = nxt_re, nxt_im
        carry_scr[b, lg, 0:1, :] = cur_re
        carry_scr[b, lg, 1:2, :] = cur_im
        recurrence(b, lg, a_re, a_im, (x0_scr[b, lg, 0], x0_scr[b, lg, 1]), True)

    def project_out(b, lg):
        y_scr[b, :, lg * LG_CHANNELS:(lg + 1) * LG_CHANNELS] = _dot(xs_scr[b, lg], wc_ref[lg])

    def finish(b):
        y = y_scr[b] + d_ref[...] * u_perm[b]
        z = _dot(_gelu(y).astype(BF16), wglu_ref[...]) + bglu_ref[...]
        ssm = z[:, :SSM_WIDTH] * jax.nn.sigmoid(z[:, SSM_WIDTH:])
        o_ref[b] = _dot(permt_ref[...], ssm.astype(BF16)).astype(BF16)

    reorder(0)
    project_in(0, 0)
    for lg in range(SSM_LANE_GROUPS):
        for b in range(n_seq):
            scan(b, lg)
            if lg == 0 and b + 1 < n_seq:
                reorder(b + 1)
                project_in(b + 1, 0)
            if lg + 1 < SSM_LANE_GROUPS:
                project_in(b, lg + 1)
            project_out(b, lg)
            if lg + 1 == SSM_LANE_GROUPS:
                finish(b)


def _ssm(u, perm, permt, wb, wc, a, apow, d, wglu, bglu):
    bsz, seq, _ = u.shape
    per_seq = lambda *shape, dt=F32: pltpu.VMEM((bsz,) + shape, dt)
    return pl.pallas_call(
        functools.partial(_ssm_kernel, n_seq=bsz),
        grid=(seq // SCAN_ROWS,),
        in_specs=[
            pl.BlockSpec((bsz, SCAN_ROWS, SSM_WIDTH), lambda s: (0, s, 0)),
            _const_spec((SCAN_ROWS, SCAN_ROWS)),
            _const_spec((SCAN_ROWS, SCAN_ROWS)),
            _const_spec((SSM_LANE_GROUPS, LG_CHANNELS, 2 * LG_STATES)),
            _const_spec((SSM_LANE_GROUPS, 2 * LG_STATES, LG_CHANNELS)),
            _const_spec((SSM_LANE_GROUPS, 2, LG_STATES)),
            _const_spec((SSM_LANE_GROUPS, 2, LG_STATES)),
            _const_spec((1, SSM_WIDTH)),
            _const_spec((SSM_WIDTH, 2 * SSM_WIDTH)),
            _const_spec((1, 2 * SSM_WIDTH)),
        ],
        out_specs=pl.BlockSpec((bsz, SCAN_ROWS, SSM_WIDTH), lambda s: (0, s, 0)),
        out_shape=jax.ShapeDtypeStruct((bsz, seq, SSM_WIDTH), BF16),
        scratch_shapes=[
            per_seq(SSM_LANE_GROUPS, 2, LG_STATES),
            per_seq(SSM_LANE_GROUPS, SCAN_ROWS, 2 * LG_STATES),
            per_seq(SSM_LANE_GROUPS, SCAN_ROWS, 2 * LG_STATES, dt=BF16),
            per_seq(SSM_LANE_GROUPS, 2, SCAN_CHUNKS, LG_STATES),
            per_seq(SCAN_ROWS, SSM_WIDTH),
        ],
        compiler_params=pltpu.CompilerParams(
            dimension_semantics=("arbitrary",), vmem_limit_bytes=SSM_VMEM),
        name="ssm",
    )(u, perm, permt, wb, wc, a, apow, d, wglu, bglu)


def _ssm_params(a_re, a_im, log_dt, b_re, b_im, c_re, c_im):
    dt = jnp.exp(log_dt)[:, None]
    mag = jnp.exp(a_re * dt)
    ab_re = mag * jnp.cos(a_im * dt)
    ab_im = mag * jnp.sin(a_im * dt)
    nr = ab_re - 1.0
    ni = ab_im
    den = a_re * a_re + a_im * a_im
    z_re = (nr * a_re + ni * a_im) / den
    z_im = (ni * a_re - nr * a_im) / den
    bb_re = z_re[..., None] * b_re - z_im[..., None] * b_im
    bb_im = z_re[..., None] * b_im + z_im[..., None] * b_re

    gl = SSM_GROUPS // SSM_LANE_GROUPS
    eye = jnp.eye(gl, dtype=F32)

    def in_mat(bb):
        bb = bb.reshape(SSM_LANE_GROUPS, gl, SSM_STATE, SSM_GROUP)
        m = jnp.einsum('lgph,gk->lghkp', bb, eye)
        return m.reshape(SSM_LANE_GROUPS, LG_CHANNELS, LG_STATES)

    def out_mat(cc):
        cc = cc.reshape(SSM_LANE_GROUPS, gl, SSM_GROUP, SSM_STATE)
        m = jnp.einsum('lghp,gk->lgpkh', cc, eye)
        return m.reshape(SSM_LANE_GROUPS, LG_STATES, LG_CHANNELS)

    wb = jnp.concatenate([in_mat(bb_re), in_mat(bb_im)], axis=-1).astype(BF16)
    wc = jnp.concatenate([out_mat(c_re), out_mat(-c_im)], axis=-2).astype(BF16)

    pr, pi = ab_re, ab_im
    for _ in range(int(math.log2(SCAN_LEN))):
        pr, pi = pr * pr - pi * pi, 2.0 * pr * pi
    lanes = lambda v: v.reshape(SSM_LANE_GROUPS, LG_STATES)
    a = jnp.stack([lanes(ab_re), lanes(ab_im)], axis=1)
    apow = jnp.stack([lanes(pr), lanes(pi)], axis=1)
    return wb, wc, a, apow


def _merge_kernel(attn_ref, ssm_ref, ga_ref, gs_ref, x_ref, wba_ref, wbs_ref, wout_ref,
                  g_ref, x1_ref, h2_ref):
    a = _dot(attn_ref[...], wba_ref[...])
    s = _dot(ssm_ref[...], wbs_ref[...])
    merged = jax.nn.sigmoid(ga_ref[...]) * a + jax.nn.sigmoid(gs_ref[...]) * s
    x1 = x_ref[...] + _dot(merged.astype(BF16), wout_ref[...])
    x1_ref[...] = x1
    h2_ref[...] = _rms_norm(x1, g_ref[...]).astype(BF16)


def _merge(attn, ssm, ga, gs, x2, wba, wbs, wout, g, tm):
    t = x2.shape[0]
    row = lambda width: pl.BlockSpec((tm, width), lambda i: (i, 0))
    return pl.pallas_call(
        _merge_kernel,
        grid=(t // tm,),
        in_specs=[row(ATTN_WIDTH), row(SSM_WIDTH), row(D_MODEL), row(D_MODEL), row(D_MODEL),
                  _const_spec((ATTN_WIDTH, D_MODEL)), _const_spec((SSM_WIDTH, D_MODEL)),
                  _const_spec((D_MODEL, D_MODEL)), _const_spec((1, D_MODEL))],
        out_specs=[row(D_MODEL), row(D_MODEL)],
        out_shape=[jax.ShapeDtypeStruct((t, D_MODEL), F32),
                   jax.ShapeDtypeStruct((t, D_MODEL), BF16)],
        compiler_params=pltpu.CompilerParams(
            dimension_semantics=("arbitrary",), vmem_limit_bytes=MERGE_VMEM),
        name="merge",
    )(attn, ssm, ga, gs, x2, wba, wbs, wout, g)


FFN_TAIL = V7X_SUBLANES
FFN_CHUNK = 256
FFN_ROW_GROUPS = 2


def _ffn_kernel(h_ref, x1_hbm, wv_ref, wg_ref, cw_ref, cb_ref, wd_ref, gf_ref,
                o_ref, x1_scr, act_scr, tail_scr, x1_sem, *, tm, tf, seq):
    i = pl.program_id(0)
    j = pl.program_id(1)
    last = pl.num_programs(1) - 1

    def x1_copy():
        return pltpu.make_async_copy(x1_hbm.at[pl.ds(i * tm, tm), :], x1_scr, x1_sem)

    @pl.when(jnp.logical_and(i == 0, j == 0))
    def _():
        tail_scr[...] = jnp.zeros_like(tail_scr)

    @pl.when(j == 0)
    def _():
        x1_copy().start()
        o_ref[...] = jnp.zeros_like(o_ref)

    seq_start = (i * tm) % seq == 0
    row = lax.broadcasted_iota(jnp.int32, (FFN_TAIL, FFN_CHUNK), 0)
    n_chunks = tf // FFN_CHUNK
    rh = tm // FFN_ROW_GROUPS
    col = lambda c: slice(c * FFN_CHUNK, (c + 1) * FFN_CHUNK)
    ups = {}
    for r in range(FFN_ROW_GROUPS):
        h = h_ref[r * rh:(r + 1) * rh, :]
        for c in range(n_chunks):
            ups[r, c] = (_dot(h, wv_ref[:, col(c)]), _dot(h, wg_ref[:, col(c)]))
    for c in range(n_chunks):
        prev = jnp.where(seq_start, 0.0, tail_scr[j, :, col(c)])
        for r in range(FFN_ROW_GROUPS):
            val, gate = ups[r, c]
            conv = cb_ref[:, col(c)] + cw_ref[CONV_WIDTH - 1:CONV_WIDTH, col(c)] * gate
            for back in range(1, CONV_WIDTH):
                shifted = pltpu.roll(gate, back, 0)
                top = jnp.where(row < back, pltpu.roll(prev, back, 0), shifted[:FFN_TAIL])
                shifted = jnp.concatenate([top, shifted[FFN_TAIL:]], axis=0)
                k = CONV_WIDTH - 1 - back
                conv = conv + cw_ref[k:k + 1, col(c)] * shifted
            prev = gate[rh - FFN_TAIL:, :]
            act_scr[r * rh:(r + 1) * rh, col(c)] = (val * _gelu(conv)).astype(BF16)
        tail_scr[j, :, col(c)] = prev
    for r in range(FFN_ROW_GROUPS):
        rows = slice(r * rh, (r + 1) * rh)
        o_ref[rows, :] += _dot(act_scr[rows, :], wd_ref[...])

    @pl.when(j == last)
    def _():
        x1_copy().wait()
        o_ref[...] = _rms_norm(x1_scr[...] + o_ref[...], gf_ref[...])


def _ffn(h2, x1, w_up, conv_w, conv_b, w_down, gf, seq, tm, tf):
    t = h2.shape[0]
    nf = D_FF // tf
    kern = functools.partial(_ffn_kernel, tm=tm, tf=tf, seq=seq)
    return pl.pallas_call(
        kern,
        grid=(t // tm, nf),
        in_specs=[
            pl.BlockSpec((tm, D_MODEL), lambda i, j: (i, 0)),
            pl.BlockSpec(memory_space=pl.ANY),
            pl.BlockSpec((D_MODEL, tf), lambda i, j: (0, j)),
            pl.BlockSpec((D_MODEL, tf), lambda i, j: (0, j + nf)),
            pl.BlockSpec((CONV_WIDTH, tf), lambda i, j: (0, j)),
            pl.BlockSpec((1, tf), lambda i, j: (0, j)),
            pl.BlockSpec((tf, D_MODEL), lambda i, j: (j, 0)),
            _const_spec((1, D_MODEL)),
        ],
        out_specs=pl.BlockSpec((tm, D_MODEL), lambda i, j: (i, 0)),
        out_shape=jax.ShapeDtypeStruct((t, D_MODEL), F32),
        scratch_shapes=[pltpu.VMEM((tm, D_MODEL), F32),
                        pltpu.VMEM((tm, tf), BF16),
                        pltpu.VMEM((nf, FFN_TAIL, tf), F32),
                        pltpu.SemaphoreType.DMA(())],
        compiler_params=pltpu.CompilerParams(
            dimension_semantics=("arbitrary", "arbitrary"), vmem_limit_bytes=FFN_VMEM),
        name="ffn",
    )(h2, x1, w_up, w_up, conv_w, conv_b, w_down, gf)


def _layer(layer, x, attn_norm_g, w_in, b_in, attn_sinks, ssm_a_re, ssm_a_im, ssm_log_dt,
           ssm_b_re, ssm_b_im, ssm_c_re, ssm_c_im, ssm_d, b_glu, ffn_norm_g, conv_w, conv_b,
           out_norm_g, stacked_weights):
    bsz, seq, width = x.shape
    t = bsz * seq
    assert width == D_MODEL and x.dtype == F32
    assert seq % ATTN_TQ == 0 and seq % SCAN_ROWS == 0 and seq % FFN_ROWS == 0
    assert t % IN_PROJ_ROWS == 0 and t % MERGE_ROWS == 0
    assert D_FF % FFN_DFF_TILE == 0 and FFN_DFF_TILE % FFN_CHUNK == 0
    x2 = x.reshape(t, D_MODEL)
    row = lambda v: v.reshape(1, -1)

    (q, kv, u, ga, gs), (w_glu, w_branch_attn, w_branch_ssm, w_out, w_up, w_down) = _in_proj(
        x2, row(attn_norm_g), w_in, row(b_in), stacked_weights, layer, tm=IN_PROJ_ROWS)

    attn = _attention(q.reshape(bsz, seq, ATTN_WIDTH), kv.reshape(bsz, seq, 2 * KV_WIDTH),
                      _attn_bias_table(attn_sinks))

    wb, wc, a, apow = _ssm_params(ssm_a_re, ssm_a_im, ssm_log_dt, ssm_b_re, ssm_b_im,
                                  ssm_c_re, ssm_c_im)
    perm = _scan_permutation()
    ssm = _ssm(u.reshape(bsz, seq, SSM_WIDTH), jnp.asarray(perm, BF16), jnp.asarray(perm.T, BF16),
               wb, wc, a, apow, row(ssm_d), w_glu, row(b_glu))

    x1, h2 = _merge(attn.reshape(t, ATTN_WIDTH), ssm.reshape(t, SSM_WIDTH), ga, gs, x2,
                    w_branch_attn, w_branch_ssm, w_out, row(ffn_norm_g), tm=MERGE_ROWS)

    out = _ffn(h2, x1, w_up, conv_w, row(conv_b), w_down, row(out_norm_g),
               seq=seq, tm=FFN_ROWS, tf=FFN_DFF_TILE)
    return out.reshape(bsz, seq, D_MODEL)


def kernel(x, attn_norm_g, w_in, b_in, attn_sinks, ssm_a_re, ssm_a_im, ssm_log_dt, ssm_b_re,
           ssm_b_im, ssm_c_re, ssm_c_im, ssm_d, w_glu, b_glu, w_branch_attn, w_branch_ssm, w_out,
           ffn_norm_g, w_up, conv_w, conv_b, w_down, final_norm_g):
    depth = w_in.shape[0]
    assert depth == 1, "the fused final norm assumes a single layer"
    return _layer(0, x, attn_norm_g[0], w_in, b_in[0], attn_sinks[0], ssm_a_re[0], ssm_a_im[0],
                  ssm_log_dt[0], ssm_b_re[0], ssm_b_im[0], ssm_c_re[0], ssm_c_im[0], ssm_d[0],
                  b_glu[0], ffn_norm_g[0], conv_w[0], conv_b[0], final_norm_g,
                  [w_glu, w_branch_attn, w_branch_ssm, w_out, w_up, w_down])
```

```python
import functools
import math

import jax
import jax.numpy as jnp
import numpy as np
from jax import lax
from jax.experimental import pallas as pl
from jax.experimental.pallas import tpu as pltpu

D_MODEL = 2048
N_Q_HEADS = 16
N_KV_HEADS = 2
HEAD_DIM = 64
Q_PER_KV = N_Q_HEADS // N_KV_HEADS
WINDOW = 128
BLOCK = 128
ATTN_WIDTH = N_Q_HEADS * HEAD_DIM
KV_WIDTH = N_KV_HEADS * HEAD_DIM
SSM_GROUP = 16
SSM_GROUPS = 32
SSM_WIDTH = SSM_GROUP * SSM_GROUPS
SSM_STATE = 64
D_FF = 5632
CONV_WIDTH = 3
RMS_EPS = 1e-6
IN_COLS = ATTN_WIDTH + 2 * KV_WIDTH + SSM_WIDTH + 2 * D_MODEL
NEG_BIG = -1e30

V7X_LANES = 128
V7X_SUBLANES = 8
V7X_VMEM_BYTES = 64 * 1024 * 1024
MIB = 1024 * 1024

BF16 = jnp.bfloat16
F32 = jnp.float32

IN_PROJ_ROWS = 256
MERGE_ROWS = 256
FFN_ROWS = 1024
FFN_DFF_TILE = 512
IN_PROJ_VMEM = 58 * MIB
ATTN_VMEM = 32 * MIB
SSM_VMEM = 48 * MIB
MERGE_VMEM = 56 * MIB
FFN_VMEM = 60 * MIB
assert max(IN_PROJ_VMEM, ATTN_VMEM, SSM_VMEM, MERGE_VMEM, FFN_VMEM) < V7X_VMEM_BYTES

_Q0 = 0
_K0 = ATTN_WIDTH
_V0 = _K0 + KV_WIDTH
_U0 = _V0 + KV_WIDTH
_GA0 = _U0 + SSM_WIDTH
_GS0 = _GA0 + D_MODEL

SCAN_CHUNKS = V7X_SUBLANES
SCAN_LEN = 64
SCAN_ROWS = SCAN_CHUNKS * SCAN_LEN
SSM_LANE_GROUPS = 4
LG_STATES = SSM_GROUPS * SSM_STATE // SSM_LANE_GROUPS
LG_CHANNELS = SSM_WIDTH // SSM_LANE_GROUPS


def _dot(a, b):
    return jnp.dot(a, b, preferred_element_type=F32)


def _dot_nt(a, b):
    return lax.dot_general(a, b, (((1,), (1,)), ((), ())), preferred_element_type=F32)


def _rms_norm(x, g):
    return x * lax.rsqrt(jnp.mean(x * x, axis=-1, keepdims=True) + RMS_EPS) * g


def _gelu(x):
    return 0.5 * x * (1.0 + lax.erf(x * np.float32(math.sqrt(0.5))))


def _const_spec(shape):
    nd = len(shape)
    return pl.BlockSpec(shape, lambda *_: (0,) * nd, pipeline_mode=pl.Buffered(1))


N_PROJ_OUT = 5
N_PROJ_SCRATCH = 3
W_IN_STAGE_ROWS = 128


def _in_proj_kernel(x_ref, g_ref, w_hbm, b_ref, *refs, layer):
    n_cast = (len(refs) - N_PROJ_OUT - N_PROJ_SCRATCH) // 2
    cast_in = refs[:n_cast]
    q_ref, kv_ref, u_ref, ga_ref, gs_ref = refs[n_cast:n_cast + N_PROJ_OUT]
    cast_out = refs[n_cast + N_PROJ_OUT:2 * n_cast + N_PROJ_OUT]
    w_scr, stage, sems = refs[2 * n_cast + N_PROJ_OUT:]

    @pl.when(pl.program_id(0) == 0)
    def _():
        n_slices = D_MODEL // W_IN_STAGE_ROWS

        def fetch(c):
            rows = pl.ds(c * W_IN_STAGE_ROWS, W_IN_STAGE_ROWS)
            return pltpu.make_async_copy(w_hbm.at[layer, rows, :], stage.at[c % 2], sems.at[c % 2])

        fetch(0).start()
        for c in range(n_slices):
            if c + 1 < n_slices:
                fetch(c + 1).start()
            fetch(c).wait()
            w_scr[c * W_IN_STAGE_ROWS:(c + 1) * W_IN_STAGE_ROWS, :] = stage[c % 2].astype(BF16)

    h = _rms_norm(x_ref[...], g_ref[...]).astype(BF16)
    proj = _dot(h, w_scr[...]) + b_ref[...]
    q_ref[...] = (proj[:, _Q0:_K0] * (HEAD_DIM ** -0.5)).astype(BF16)
    kv_ref[...] = proj[:, _K0:_U0].astype(BF16)
    u_ref[...] = proj[:, _U0:_GA0]
    ga_ref[...] = proj[:, _GA0:_GS0]
    gs_ref[...] = proj[:, _GS0:]
    for src, dst in zip(cast_in, cast_out):
        dst[...] = src[...].astype(BF16)


def _cast_tiling(n_rows, n_steps):
    bf16_rows = 2 * V7X_SUBLANES
    for rows in range(bf16_rows, n_rows + 1, bf16_rows):
        if n_rows % rows == 0 and n_rows // rows <= n_steps and n_steps % (n_rows // rows) == 0:
            return rows, n_steps // (n_rows // rows)
    raise ValueError(f"no cast tiling for {n_rows} rows over {n_steps} steps")


def _in_proj(x2, g, w, b, cast_weights, layer, tm):
    t = x2.shape[0]
    n_steps = t // tm
    row = lambda width: pl.BlockSpec((tm, width), lambda i: (i, 0))
    cast_in_specs, cast_out_specs, cast_shapes = [], [], []
    for cw in cast_weights:
        _, n_rows, n_cols = cw.shape
        rows, hold = _cast_tiling(n_rows, n_steps)
        cast_in_specs.append(pl.BlockSpec((None, rows, n_cols),
                                          lambda i, hold=hold: (layer, i // hold, 0)))
        cast_out_specs.append(pl.BlockSpec((rows, n_cols), lambda i, hold=hold: (i // hold, 0)))
        cast_shapes.append(jax.ShapeDtypeStruct((n_rows, n_cols), BF16))
    outs = pl.pallas_call(
        functools.partial(_in_proj_kernel, layer=layer),
        grid=(n_steps,),
        in_specs=[row(D_MODEL), _const_spec((1, D_MODEL)),
                  pl.BlockSpec(memory_space=pl.ANY), _const_spec((1, IN_COLS))] + cast_in_specs,
        out_specs=[row(ATTN_WIDTH), row(2 * KV_WIDTH), row(SSM_WIDTH),
                   row(D_MODEL), row(D_MODEL)] + cast_out_specs,
        out_shape=[jax.ShapeDtypeStruct((t, ATTN_WIDTH), BF16),
                   jax.ShapeDtypeStruct((t, 2 * KV_WIDTH), BF16),
                   jax.ShapeDtypeStruct((t, SSM_WIDTH), F32),
                   jax.ShapeDtypeStruct((t, D_MODEL), F32),
                   jax.ShapeDtypeStruct((t, D_MODEL), F32)] + cast_shapes,
        scratch_shapes=[pltpu.VMEM((D_MODEL, IN_COLS), BF16),
                        pltpu.VMEM((2, W_IN_STAGE_ROWS, IN_COLS), F32),
                        pltpu.SemaphoreType.DMA((2,))],
        compiler_params=pltpu.CompilerParams(
            dimension_semantics=("arbitrary",), vmem_limit_bytes=IN_PROJ_VMEM),
        name="in_proj",
    )(x2, g, w, b, *cast_weights)
    return outs[:N_PROJ_OUT], outs[N_PROJ_OUT:]


ATTN_TQ = 1024
ATTN_SUB = ATTN_TQ // BLOCK
ATTN_GROUP = 8


def _attn_bias_table(sinks):
    q_idx = np.arange(BLOCK)[:, None]
    s_idx = np.arange(2 * BLOCK)[None, :]
    dist = q_idx + BLOCK - s_idx
    band = (dist >= 0) & (dist < WINDOW)
    assert not band[:, 0].any()
    slopes = 2.0 ** (-8.0 * np.arange(1, N_Q_HEADS + 1, dtype=np.float32) / N_Q_HEADS)
    alibi = -slopes[:, None, None].astype(np.float32) * dist.astype(np.float32)[None]
    general = np.where(band[None], alibi, np.float32(NEG_BIG))
    first = np.where((band & (s_idx >= BLOCK))[None], alibi, np.float32(NEG_BIG))
    table = jnp.asarray(np.stack([general, first]).astype(np.float32))
    sink_col = jnp.asarray((s_idx == 0)[None, None])
    return jnp.where(sink_col, sinks.astype(F32)[None, :, None, None], table)


ATTN_PATCH = 16


def _attn_kernel(q_ref, kvc_ref, kvp_ref, bias_ref, o_ref, ops_scr):
    n = pl.program_id(1)
    lane = lax.broadcasted_iota(jnp.int32, (1, 2 * HEAD_DIM), 1)
    lo = lane < HEAD_DIM

    def operands(kv):
        kv = kv.astype(F32)
        k_nat = kv[:, :KV_WIDTH]
        v_nat = kv[:, KV_WIDTH:]
        k_swp = pltpu.roll(k_nat, HEAD_DIM, 1)
        v_swp = pltpu.roll(v_nat, HEAD_DIM, 1)
        return (jnp.where(lo, k_nat, 0.0), jnp.where(lo, k_swp, 0.0),
                jnp.where(lo, 0.0, k_swp), jnp.where(lo, 0.0, k_nat),
                jnp.where(lo, v_nat, 1.0), jnp.where(lo, v_swp, 1.0),
                jnp.where(lo, 1.0, v_swp), jnp.where(lo, 1.0, v_nat))

    for idx, op in enumerate(operands(kvp_ref[...])):
        ops_scr[idx, 0:BLOCK, :] = op.astype(BF16)
    for idx, op in enumerate(operands(kvc_ref[...])):
        ops_scr[idx, BLOCK:, :] = op.astype(BF16)

    sink_row = lax.broadcasted_iota(jnp.int32, (ATTN_PATCH, 1), 0) == 0
    zero = jnp.zeros((1, 2 * HEAD_DIM), F32)
    sink_fill = (zero, zero, zero, zero,
                 jnp.where(lo, 0.0, 1.0), jnp.where(lo, 0.0, 1.0),
                 jnp.where(lo, 1.0, 0.0), jnp.where(lo, 1.0, 0.0))

    def window(idx, r0):
        top = ops_scr[idx, pl.ds(r0, ATTN_PATCH), :].astype(F32)
        top = jnp.where(sink_row, sink_fill[idx], top).astype(BF16)
        rest = ops_scr[idx, pl.ds(r0 + ATTN_PATCH, 2 * BLOCK - ATTN_PATCH), :]
        return jnp.concatenate([top, rest], axis=0)

    pairs_per_group = Q_PER_KV // 2
    group_pairs = [range(grp * pairs_per_group, (grp + 1) * pairs_per_group)
                   for grp in range(N_KV_HEADS)]

    def sub_blocks(it, carry):
        r0s = [pl.multiple_of((it * ATTN_GROUP + k) * BLOCK, BLOCK) for k in range(ATTN_GROUP)]
        wins, scores = {}, {}
        for k, r0 in enumerate(r0s):
            first = jnp.where(jnp.logical_and(n == 0, it * ATTN_GROUP + k == 0), 1, 0)
            wins[k] = [window(idx, r0) for idx in range(8)]
            q_all = q_ref[pl.ds(r0, BLOCK), :]
            for grp, pairs in enumerate(group_pairs):
                q_stack = jnp.concatenate(
                    [q_all[:, pr * 2 * HEAD_DIM:(pr + 1) * 2 * HEAD_DIM] for pr in pairs], axis=0)
                for odd in range(2):
                    bias = jnp.concatenate(
                        [bias_ref[first, 2 * pr + odd] for pr in pairs], axis=0)
                    scores[k, grp, odd] = _dot_nt(q_stack, wins[k][2 * odd + grp]) + bias
        probs = {}
        for key, s in scores.items():
            m = jnp.max(s, axis=-1, keepdims=True)
            probs[key] = jnp.exp(s - m).astype(BF16)
        res = {(k, grp, odd): _dot(p, wins[k][4 + 2 * odd + grp])
               for (k, grp, odd), p in probs.items()}
        for k, r0 in enumerate(r0s):
            for grp, pairs in enumerate(group_pairs):
                num = jnp.where(lo, res[k, grp, 0], res[k, grp, 1])
                den = pltpu.roll(jnp.where(lo, res[k, grp, 1], res[k, grp, 0]), HEAD_DIM, 1)
                out = (num / den).astype(BF16)
                for j, pr in enumerate(pairs):
                    o_ref[pl.ds(r0, BLOCK), pr * 2 * HEAD_DIM:(pr + 1) * 2 * HEAD_DIM] = (
                        out[j * BLOCK:(j + 1) * BLOCK, :])
        return carry

    lax.fori_loop(0, ATTN_SUB // ATTN_GROUP, sub_blocks, 0)


def _attention(q, kv, bias):
    bsz, seq, _ = q.shape
    nq = seq // ATTN_TQ
    return pl.pallas_call(
        _attn_kernel,
        grid=(bsz, nq),
        in_specs=[
            pl.BlockSpec((None, ATTN_TQ, ATTN_WIDTH), lambda b, n: (b, n, 0)),
            pl.BlockSpec((None, ATTN_TQ, 2 * KV_WIDTH), lambda b, n: (b, n, 0)),
            pl.BlockSpec((None, BLOCK, 2 * KV_WIDTH),
                         lambda b, n: (b, jnp.maximum(n * ATTN_SUB - 1, 0), 0)),
            _const_spec((2, N_Q_HEADS, BLOCK, 2 * BLOCK)),
        ],
        out_specs=pl.BlockSpec((None, ATTN_TQ, ATTN_WIDTH), lambda b, n: (b, n, 0)),
        out_shape=jax.ShapeDtypeStruct((bsz, seq, ATTN_WIDTH), BF16),
        scratch_shapes=[pltpu.VMEM((8, ATTN_TQ + BLOCK, 2 * HEAD_DIM), BF16)],
        compiler_params=pltpu.CompilerParams(
            dimension_semantics=("arbitrary", "arbitrary"), vmem_limit_bytes=ATTN_VMEM),
        name="attn",
    )(q, kv, kv, bias)


def _scan_permutation():
    p = np.zeros((SCAN_ROWS, SCAN_ROWS), np.float32)
    for c in range(SCAN_CHUNKS):
        for s in range(SCAN_LEN):
            p[s * SCAN_CHUNKS + c, c * SCAN_LEN + s] = 1.0
    return p


def _ssm_kernel(u_ref, perm_ref, permt_ref, wb_ref, wc_ref, a_ref, apow_ref, d_ref,
                wglu_ref, bglu_ref, o_ref, carry_scr, bu_scr, xs_scr, x0_scr, y_scr, *, n_seq):
    @pl.when(pl.program_id(0) == 0)
    def _():
        carry_scr[...] = jnp.zeros_like(carry_scr)

    perm = perm_ref[...]
    u_perm, ub = {}, {}

    def reorder(b):
        u = u_ref[b]
        u_hi = u.astype(BF16)
        u_lo = (u - u_hi.astype(F32)).astype(BF16)
        up_hi = _dot(perm, u_hi)
        u_perm[b] = up_hi + _dot(perm, u_lo)
        ub[b] = up_hi.astype(BF16)

    def project_in(b, lg):
        bu_scr[b, lg] = _dot(ub[b][:, lg * LG_CHANNELS:(lg + 1) * LG_CHANNELS], wb_ref[lg])

    def recurrence(b, lg, a_re, a_im, state, store):
        xr, xi = state
        for s in range(0, SCAN_LEN, 2):
            pair_re, pair_im = [], []
            for t in (s, s + 1):
                rows = slice(t * SCAN_CHUNKS, (t + 1) * SCAN_CHUNKS)
                br = bu_scr[b, lg, rows, :LG_STATES]
                bi = bu_scr[b, lg, rows, LG_STATES:]
                xr, xi = a_re * xr - a_im * xi + br, a_re * xi + a_im * xr + bi
                pair_re.append(xr)
                pair_im.append(xi)
            if store:
                rows = slice(s * SCAN_CHUNKS, (s + 2) * SCAN_CHUNKS)
                xs_scr[b, lg, rows, :LG_STATES] = jnp.concatenate(pair_re, axis=0).astype(BF16)
                xs_scr[b, lg, rows, LG_STATES:] = jnp.concatenate(pair_im, axis=0).astype(BF16)
        return xr, xi

    def scan(b, lg):
        a_re = jnp.broadcast_to(a_ref[lg, 0:1, :], (SCAN_CHUNKS, LG_STATES))
        a_im = jnp.broadcast_to(a_ref[lg, 1:2, :], (SCAN_CHUNKS, LG_STATES))
        zeros = jnp.zeros((SCAN_CHUNKS, LG_STATES), F32)
        e_re, e_im = recurrence(b, lg, a_re, a_im, (zeros, zeros), False)
        p_re = apow_ref[lg, 0:1, :]
        p_im = apow_ref[lg, 1:2, :]
        cur_re = carry_scr[b, lg, 0:1, :]
        cur_im = carry_scr[b, lg, 1:2, :]
        for c in range(SCAN_CHUNKS):
            x0_scr[b, lg, 0, c:c + 1, :] = cur_re
            x0_scr[b, lg, 1, c:c + 1, :] = cur_im
            nxt_re = p_re * cur_re - p_im * cur_im + e_re[c:c + 1, :]
            nxt_im = p_re * cur_im + p_im * cur_re + e_im[c:c + 1, :]
            cur_re, cur_im = nxt_re, nxt_im
        carry_scr[b, lg, 0:1, :] = cur_re
        carry_scr[b, lg, 1:2, :] = cur_im
        recurrence(b, lg, a_re, a_im, (x0_scr[b, lg, 0], x0_scr[b, lg, 1]), True)

    def project_out(b, lg):
        y_scr[b, :, lg * LG_CHANNELS:(lg + 1) * LG_CHANNELS] = _dot(xs_scr[b, lg], wc_ref[lg])

    def finish(b):
        y = y_scr[b] + d_ref[...] * u_perm[b]
        z = _dot(_gelu(y).astype(BF16), wglu_ref[...]) + bglu_ref[...]
        ssm = z[:, :SSM_WIDTH] * jax.nn.sigmoid(z[:, SSM_WIDTH:])
        o_ref[b] = _dot(permt_ref[...], ssm.astype(BF16)).astype(BF16)

    reorder(0)
    project_in(0, 0)
    for lg in range(SSM_LANE_GROUPS):
        for b in range(n_seq):
            scan(b, lg)
            if lg == 0 and b + 1 < n_seq:
                reorder(b + 1)
                project_in(b + 1, 0)
            if lg + 1 < SSM_LANE_GROUPS:
                project_in(b, lg + 1)
            project_out(b, lg)
            if lg + 1 == SSM_LANE_GROUPS:
                finish(b)


def _ssm(u, perm, permt, wb, wc, a, apow, d, wglu, bglu):
    bsz, seq, _ = u.shape
    per_seq = lambda *shape, dt=F32: pltpu.VMEM((bsz,) + shape, dt)
    return pl.pallas_call(
        functools.partial(_ssm_kernel, n_seq=bsz),
        grid=(seq // SCAN_ROWS,),
        in_specs=[
            pl.BlockSpec((bsz, SCAN_ROWS, SSM_WIDTH), lambda s: (0, s, 0)),
            _const_spec((SCAN_ROWS, SCAN_ROWS)),
            _const_spec((SCAN_ROWS, SCAN_ROWS)),
            _const_spec((SSM_LANE_GROUPS, LG_CHANNELS, 2 * LG_STATES)),
            _const_spec((SSM_LANE_GROUPS, 2 * LG_STATES, LG_CHANNELS)),
            _const_spec((SSM_LANE_GROUPS, 2, LG_STATES)),
            _const_spec((SSM_LANE_GROUPS, 2, LG_STATES)),
            _const_spec((1, SSM_WIDTH)),
            _const_spec((SSM_WIDTH, 2 * SSM_WIDTH)),
            _const_spec((1, 2 * SSM_WIDTH)),
        ],
        out_specs=pl.BlockSpec((bsz, SCAN_ROWS, SSM_WIDTH), lambda s: (0, s, 0)),
        out_shape=jax.ShapeDtypeStruct((bsz, seq, SSM_WIDTH), BF16),
        scratch_shapes=[
            per_seq(SSM_LANE_GROUPS, 2, LG_STATES),
            per_seq(SSM_LANE_GROUPS, SCAN_ROWS, 2 * LG_STATES),
            per_seq(SSM_LANE_GROUPS, SCAN_ROWS, 2 * LG_STATES, dt=BF16),
            per_seq(SSM_LANE_GROUPS, 2, SCAN_CHUNKS, LG_STATES),
            per_seq(SCAN_ROWS, SSM_WIDTH),
        ],
        compiler_params=pltpu.CompilerParams(
            dimension_semantics=("arbitrary",), vmem_limit_bytes=SSM_VMEM),
        name="ssm",
    )(u, perm, permt, wb, wc, a, apow, d, wglu, bglu)


def _ssm_params(a_re, a_im, log_dt, b_re, b_im, c_re, c_im):
    dt = jnp.exp(log_dt)[:, None]
    mag = jnp.exp(a_re * dt)
    ab_re = mag * jnp.cos(a_im * dt)
    ab_im = mag * jnp.sin(a_im * dt)
    nr = ab_re - 1.0
    ni = ab_im
    den = a_re * a_re + a_im * a_im
    z_re = (nr * a_re + ni * a_im) / den
    z_im = (ni * a_re - nr * a_im) / den
    bb_re = z_re[..., None] * b_re - z_im[..., None] * b_im
    bb_im = z_re[..., None] * b_im + z_im[..., None] * b_re

    gl = SSM_GROUPS // SSM_LANE_GROUPS
    eye = jnp.eye(gl, dtype=F32)

    def in_mat(bb):
        bb = bb.reshape(SSM_LANE_GROUPS, gl, SSM_STATE, SSM_GROUP)
        m = jnp.einsum('lgph,gk->lghkp', bb, eye)
        return m.reshape(SSM_LANE_GROUPS, LG_CHANNELS, LG_STATES)

    def out_mat(cc):
        cc = cc.reshape(SSM_LANE_GROUPS, gl, SSM_GROUP, SSM_STATE)
        m = jnp.einsum('lghp,gk->lgpkh', cc, eye)
        return m.reshape(SSM_LANE_GROUPS, LG_STATES, LG_CHANNELS)

    wb = jnp.concatenate([in_mat(bb_re), in_mat(bb_im)], axis=-1).astype(BF16)
    wc = jnp.concatenate([out_mat(c_re), out_mat(-c_im)], axis=-2).astype(BF16)

    pr, pi = ab_re, ab_im
    for _ in range(int(math.log2(SCAN_LEN))):
        pr, pi = pr * pr - pi * pi, 2.0 * pr * pi
    lanes = lambda v: v.reshape(SSM_LANE_GROUPS, LG_STATES)
    a = jnp.stack([lanes(ab_re), lanes(ab_im)], axis=1)
    apow = jnp.stack([lanes(pr), lanes(pi)], axis=1)
    return wb, wc, a, apow


def _merge_kernel(attn_ref, ssm_ref, ga_ref, gs_ref, x_ref, wba_ref, wbs_ref, wout_ref,
                  g_ref, x1_ref, h2_ref):
    a = _dot(attn_ref[...], wba_ref[...])
    s = _dot(ssm_ref[...], wbs_ref[...])
    merged = jax.nn.sigmoid(ga_ref[...]) * a + jax.nn.sigmoid(gs_ref[...]) * s
    x1 = x_ref[...] + _dot(merged.astype(BF16), wout_ref[...])
    x1_ref[...] = x1
    h2_ref[...] = _rms_norm(x1, g_ref[...]).astype(BF16)


def _merge(attn, ssm, ga, gs, x2, wba, wbs, wout, g, tm):
    t = x2.shape[0]
    row = lambda width: pl.BlockSpec((tm, width), lambda i: (i, 0))
    return pl.pallas_call(
        _merge_kernel,
        grid=(t // tm,),
        in_specs=[row(ATTN_WIDTH), row(SSM_WIDTH), row(D_MODEL), row(D_MODEL), row(D_MODEL),
                  _const_spec((ATTN_WIDTH, D_MODEL)), _const_spec((SSM_WIDTH, D_MODEL)),
                  _const_spec((D_MODEL, D_MODEL)), _const_spec((1, D_MODEL))],
        out_specs=[row(D_MODEL), row(D_MODEL)],
        out_shape=[jax.ShapeDtypeStruct((t, D_MODEL), F32),
                   jax.ShapeDtypeStruct((t, D_MODEL), BF16)],
        compiler_params=pltpu.CompilerParams(
            dimension_semantics=("arbitrary",), vmem_limit_bytes=MERGE_VMEM),
        name="merge",
    )(attn, ssm, ga, gs, x2, wba, wbs, wout, g)


FFN_TAIL = V7X_SUBLANES
FFN_CHUNK = 256
FFN_ROW_GROUPS = 4


def _ffn_kernel(h_ref, x1_hbm, wv_ref, wg_ref, cw_ref, cb_ref, wd_ref, gf_ref,
                o_ref, x1_scr, act_scr, tail_scr, x1_sem, *, tm, tf, seq):
    i = pl.program_id(0)
    j = pl.program_id(1)
    last = pl.num_programs(1) - 1

    def x1_copy():
        return pltpu.make_async_copy(x1_hbm.at[pl.ds(i * tm, tm), :], x1_scr, x1_sem)

    @pl.when(jnp.logical_and(i == 0, j == 0))
    def _():
        tail_scr[...] = jnp.zeros_like(tail_scr)

    @pl.when(j == 0)
    def _():
        x1_copy().start()
        o_ref[...] = jnp.zeros_like(o_ref)

    seq_start = (i * tm) % seq == 0
    row = lax.broadcasted_iota(jnp.int32, (FFN_TAIL, FFN_CHUNK), 0)
    n_chunks = tf // FFN_CHUNK
    rh = tm // FFN_ROW_GROUPS
    col = lambda c: slice(c * FFN_CHUNK, (c + 1) * FFN_CHUNK)
    ups = {}
    for r in range(FFN_ROW_GROUPS):
        h = h_ref[r * rh:(r + 1) * rh, :]
        for c in range(n_chunks):
            ups[r, c] = (_dot(h, wv_ref[:, col(c)]), _dot(h, wg_ref[:, col(c)]))
    for c in range(n_chunks):
        prev = jnp.where(seq_start, 0.0, tail_scr[j, :, col(c)])
        for r in range(FFN_ROW_GROUPS):
            val, gate = ups[r, c]
            conv = cb_ref[:, col(c)] + cw_ref[CONV_WIDTH - 1:CONV_WIDTH, col(c)] * gate
            for back in range(1, CONV_WIDTH):
                shifted = pltpu.roll(gate, back, 0)
                top = jnp.where(row < back, pltpu.roll(prev, back, 0), shifted[:FFN_TAIL])
                shifted = jnp.concatenate([top, shifted[FFN_TAIL:]], axis=0)
                k = CONV_WIDTH - 1 - back
                conv = conv + cw_ref[k:k + 1, col(c)] * shifted
            prev = gate[rh - FFN_TAIL:, :]
            act_scr[r * rh:(r + 1) * rh, col(c)] = (val * _gelu(conv)).astype(BF16)
        tail_scr[j, :, col(c)] = prev
    for r in range(FFN_ROW_GROUPS):
        rows = slice(r * rh, (r + 1) * rh)
        o_ref[rows, :] += _dot(act_scr[rows, :], wd_ref[...])

    @pl.when(j == last)
    def _():
        x1_copy().wait()
        o_ref[...] = _rms_norm(x1_scr[...] + o_ref[...], gf_ref[...])


def _ffn(h2, x1, w_up, conv_w, conv_b, w_down, gf, seq, tm, tf):
    t = h2.shape[0]
    nf = D_FF // tf
    kern = functools.partial(_ffn_kernel, tm=tm, tf=tf, seq=seq)
    return pl.pallas_call(
        kern,
        grid=(t // tm, nf),
        in_specs=[
            pl.BlockSpec((tm, D_MODEL), lambda i, j: (i, 0)),
            pl.BlockSpec(memory_space=pl.ANY),
            pl.BlockSpec((D_MODEL, tf), lambda i, j: (0, j)),
            pl.BlockSpec((D_MODEL, tf), lambda i, j: (0, j + nf)),
            pl.BlockSpec((CONV_WIDTH, tf), lambda i, j: (0, j)),
            pl.BlockSpec((1, tf), lambda i, j: (0, j)),
            pl.BlockSpec((tf, D_MODEL), lambda i, j: (j, 0)),
            _const_spec((1, D_MODEL)),
        ],
        out_specs=pl.BlockSpec((tm, D_MODEL), lambda i, j: (i, 0)),
        out_shape=jax.ShapeDtypeStruct((t, D_MODEL), F32),
        scratch_shapes=[pltpu.VMEM((tm, D_MODEL), F32),
                        pltpu.VMEM((tm, tf), BF16),
                        pltpu.VMEM((nf, FFN_TAIL, tf), F32),
                        pltpu.SemaphoreType.DMA(())],
        compiler_params=pltpu.CompilerParams(
            dimension_semantics=("arbitrary", "arbitrary"), vmem_limit_bytes=FFN_VMEM),
        name="ffn",
    )(h2, x1, w_up, w_up, conv_w, conv_b, w_down, gf)


def _layer(layer, x, attn_norm_g, w_in, b_in, attn_sinks, ssm_a_re, ssm_a_im, ssm_log_dt,
           ssm_b_re, ssm_b_im, ssm_c_re, ssm_c_im, ssm_d, b_glu, ffn_norm_g, conv_w, conv_b,
           out_norm_g, stacked_weights):
    bsz, seq, width = x.shape
    t = bsz * seq
    assert width == D_MODEL and x.dtype == F32
    assert seq % ATTN_TQ == 0 and seq % SCAN_ROWS == 0 and seq % FFN_ROWS == 0
    assert t % IN_PROJ_ROWS == 0 and t % MERGE_ROWS == 0
    assert D_FF % FFN_DFF_TILE == 0 and FFN_DFF_TILE % FFN_CHUNK == 0
    x2 = x.reshape(t, D_MODEL)
    row = lambda v: v.reshape(1, -1)

    (q, kv, u, ga, gs), (w_glu, w_branch_attn, w_branch_ssm, w_out, w_up, w_down) = _in_proj(
        x2, row(attn_norm_g), w_in, row(b_in), stacked_weights, layer, tm=IN_PROJ_ROWS)

    attn = _attention(q.reshape(bsz, seq, ATTN_WIDTH), kv.reshape(bsz, seq, 2 * KV_WIDTH),
                      _attn_bias_table(attn_sinks))

    wb, wc, a, apow = _ssm_params(ssm_a_re, ssm_a_im, ssm_log_dt, ssm_b_re, ssm_b_im,
                                  ssm_c_re, ssm_c_im)
    perm = _scan_permutation()
    ssm = _ssm(u.reshape(bsz, seq, SSM_WIDTH), jnp.asarray(perm, BF16), jnp.asarray(perm.T, BF16),
               wb, wc, a, apow, row(ssm_d), w_glu, row(b_glu))

    x1, h2 = _merge(attn.reshape(t, ATTN_WIDTH), ssm.reshape(t, SSM_WIDTH), ga, gs, x2,
                    w_branch_attn, w_branch_ssm, w_out, row(ffn_norm_g), tm=MERGE_ROWS)

    out = _ffn(h2, x1, w_up, conv_w, row(conv_b), w_down, row(out_norm_g),
               seq=seq, tm=FFN_ROWS, tf=FFN_DFF_TILE)
    return out.reshape(bsz, seq, D_MODEL)


def kernel(x, attn_norm_g, w_in, b_in, attn_sinks, ssm_a_re, ssm_a_im, ssm_log_dt, ssm_b_re,
           ssm_b_im, ssm_c_re, ssm_c_im, ssm_d, w_glu, b_glu, w_branch_attn, w_branch_ssm, w_out,
           ffn_norm_g, w_up, conv_w, conv_b, w_down, final_norm_g):
    depth = w_in.shape[0]
    assert depth == 1, "the fused final norm assumes a single layer"
    return _layer(0, x, attn_norm_g[0], w_in, b_in[0], attn_sinks[0], ssm_a_re[0], ssm_a_im[0],
                  ssm_log_dt[0], ssm_b_re[0], ssm_b_im[0], ssm_c_re[0], ssm_c_im[0], ssm_d[0],
                  b_glu[0], ffn_norm_g[0], conv_w[0], conv_b[0], final_norm_g,
                  [w_glu, w_branch_attn, w_branch_ssm, w_out, w_up, w_down])
```

```python
import functools
import math

import jax
import jax.numpy as jnp
import numpy as np
from jax import lax
from jax.experimental import pallas as pl
from jax.experimental.pallas import tpu as pltpu

D_MODEL = 2048
N_Q_HEADS = 16
N_KV_HEADS = 2
HEAD_DIM = 64
Q_PER_KV = N_Q_HEADS // N_KV_HEADS
WINDOW = 128
BLOCK = 128
ATTN_WIDTH = N_Q_HEADS * HEAD_DIM
KV_WIDTH = N_KV_HEADS * HEAD_DIM
SSM_GROUP = 16
SSM_GROUPS = 32
SSM_WIDTH = SSM_GROUP * SSM_GROUPS
SSM_STATE = 64
D_FF = 5632
CONV_WIDTH = 3
RMS_EPS = 1e-6
IN_COLS = ATTN_WIDTH + 2 * KV_WIDTH + SSM_WIDTH + 2 * D_MODEL
NEG_BIG = -1e30

V7X_LANES = 128
V7X_SUBLANES = 8
V7X_VMEM_BYTES = 64 * 1024 * 1024
MIB = 1024 * 1024

BF16 = jnp.bfloat16
F32 = jnp.float32

IN_PROJ_ROWS = 256
MERGE_ROWS = 256
FFN_ROWS = 1024
FFN_DFF_TILE = 512
IN_PROJ_VMEM = 58 * MIB
ATTN_VMEM = 32 * MIB
SSM_VMEM = 48 * MIB
MERGE_VMEM = 56 * MIB
FFN_VMEM = 60 * MIB
assert max(IN_PROJ_VMEM, ATTN_VMEM, SSM_VMEM, MERGE_VMEM, FFN_VMEM) < V7X_VMEM_BYTES

_Q0 = 0
_K0 = ATTN_WIDTH
_V0 = _K0 + KV_WIDTH
_U0 = _V0 + KV_WIDTH
_GA0 = _U0 + SSM_WIDTH
_GS0 = _GA0 + D_MODEL

SCAN_CHUNKS = V7X_SUBLANES
SCAN_LEN = 64
SCAN_ROWS = SCAN_CHUNKS * SCAN_LEN
SSM_LANE_GROUPS = 4
LG_STATES = SSM_GROUPS * SSM_STATE // SSM_LANE_GROUPS
LG_CHANNELS = SSM_WIDTH // SSM_LANE_GROUPS


def _dot(a, b):
    return jnp.dot(a, b, preferred_element_type=F32)


def _dot_nt(a, b):
    return lax.dot_general(a, b, (((1,), (1,)), ((), ())), preferred_element_type=F32)


def _rms_norm(x, g):
    return x * lax.rsqrt(jnp.mean(x * x, axis=-1, keepdims=True) + RMS_EPS) * g


def _gelu(x):
    return 0.5 * x * (1.0 + lax.erf(x * np.float32(math.sqrt(0.5))))


def _const_spec(shape):
    nd = len(shape)
    return pl.BlockSpec(shape, lambda *_: (0,) * nd, pipeline_mode=pl.Buffered(1))


N_PROJ_OUT = 5
N_PROJ_SCRATCH = 3
W_IN_STAGE_ROWS = 128


def _in_proj_kernel(x_ref, g_ref, w_hbm, b_ref, *refs, layer):
    n_cast = (len(refs) - N_PROJ_OUT - N_PROJ_SCRATCH) // 2
    cast_in = refs[:n_cast]
    q_ref, kv_ref, u_ref, ga_ref, gs_ref = refs[n_cast:n_cast + N_PROJ_OUT]
    cast_out = refs[n_cast + N_PROJ_OUT:2 * n_cast + N_PROJ_OUT]
    w_scr, stage, sems = refs[2 * n_cast + N_PROJ_OUT:]

    @pl.when(pl.program_id(0) == 0)
    def _():
        n_slices = D_MODEL // W_IN_STAGE_ROWS

        def fetch(c):
            rows = pl.ds(c * W_IN_STAGE_ROWS, W_IN_STAGE_ROWS)
            return pltpu.make_async_copy(w_hbm.at[layer, rows, :], stage.at[c % 2], sems.at[c % 2])

        fetch(0).start()
        for c in range(n_slices):
            if c + 1 < n_slices:
                fetch(c + 1).start()
            fetch(c).wait()
            w_scr[c * W_IN_STAGE_ROWS:(c + 1) * W_IN_STAGE_ROWS, :] = stage[c % 2].astype(BF16)

    h = _rms_norm(x_ref[...], g_ref[...]).astype(BF16)
    proj = _dot(h, w_scr[:, _Q0:_GA0]) + b_ref[:, _Q0:_GA0]
    q_ref[...] = (proj[:, _Q0:_K0] * (HEAD_DIM ** -0.5)).astype(BF16)
    kv_ref[...] = proj[:, _K0:_U0].astype(BF16)
    u_ref[...] = proj[:, _U0:_GA0]
    ga_ref[...] = _dot(h, w_scr[:, _GA0:_GS0]) + b_ref[:, _GA0:_GS0]
    gs_ref[...] = _dot(h, w_scr[:, _GS0:]) + b_ref[:, _GS0:]
    for src, dst in zip(cast_in, cast_out):
        dst[...] = src[...].astype(BF16)


def _cast_tiling(n_rows, n_steps):
    bf16_rows = 2 * V7X_SUBLANES
    for rows in range(bf16_rows, n_rows + 1, bf16_rows):
        if n_rows % rows == 0 and n_rows // rows <= n_steps and n_steps % (n_rows // rows) == 0:
            return rows, n_steps // (n_rows // rows)
    raise ValueError(f"no cast tiling for {n_rows} rows over {n_steps} steps")


def _in_proj(x2, g, w, b, cast_weights, layer, tm):
    t = x2.shape[0]
    n_steps = t // tm
    row = lambda width: pl.BlockSpec((tm, width), lambda i: (i, 0))
    cast_in_specs, cast_out_specs, cast_shapes = [], [], []
    for cw in cast_weights:
        _, n_rows, n_cols = cw.shape
        rows, hold = _cast_tiling(n_rows, n_steps)
        cast_in_specs.append(pl.BlockSpec((None, rows, n_cols),
                                          lambda i, hold=hold: (layer, i // hold, 0)))
        cast_out_specs.append(pl.BlockSpec((rows, n_cols), lambda i, hold=hold: (i // hold, 0)))
        cast_shapes.append(jax.ShapeDtypeStruct((n_rows, n_cols), BF16))
    outs = pl.pallas_call(
        functools.partial(_in_proj_kernel, layer=layer),
        grid=(n_steps,),
        in_specs=[row(D_MODEL), _const_spec((1, D_MODEL)),
                  pl.BlockSpec(memory_space=pl.ANY), _const_spec((1, IN_COLS))] + cast_in_specs,
        out_specs=[row(ATTN_WIDTH), row(2 * KV_WIDTH), row(SSM_WIDTH),
                   row(D_MODEL), row(D_MODEL)] + cast_out_specs,
        out_shape=[jax.ShapeDtypeStruct((t, ATTN_WIDTH), BF16),
                   jax.ShapeDtypeStruct((t, 2 * KV_WIDTH), BF16),
                   jax.ShapeDtypeStruct((t, SSM_WIDTH), F32),
                   jax.ShapeDtypeStruct((t, D_MODEL), F32),
                   jax.ShapeDtypeStruct((t, D_MODEL), F32)] + cast_shapes,
        scratch_shapes=[pltpu.VMEM((D_MODEL, IN_COLS), BF16),
                        pltpu.VMEM((2, W_IN_STAGE_ROWS, IN_COLS), F32),
                        pltpu.SemaphoreType.DMA((2,))],
        compiler_params=pltpu.CompilerParams(
            dimension_semantics=("arbitrary",), vmem_limit_bytes=IN_PROJ_VMEM),
        name="in_proj",
    )(x2, g, w, b, *cast_weights)
    return outs[:N_PROJ_OUT], outs[N_PROJ_OUT:]


ATTN_TQ = 1024
ATTN_SUB = ATTN_TQ // BLOCK
ATTN_GROUP = 8


def _attn_bias_table(sinks):
    q_idx = np.arange(BLOCK)[:, None]
    s_idx = np.arange(2 * BLOCK)[None, :]
    dist = q_idx + BLOCK - s_idx
    band = (dist >= 0) & (dist < WINDOW)
    assert not band[:, 0].any()
    slopes = 2.0 ** (-8.0 * np.arange(1, N_Q_HEADS + 1, dtype=np.float32) / N_Q_HEADS)
    alibi = -slopes[:, None, None].astype(np.float32) * dist.astype(np.float32)[None]
    general = np.where(band[None], alibi, np.float32(NEG_BIG))
    first = np.where((band & (s_idx >= BLOCK))[None], alibi, np.float32(NEG_BIG))
    table = jnp.asarray(np.stack([general, first]).astype(np.float32))
    sink_col = jnp.asarray((s_idx == 0)[None, None])
    return jnp.where(sink_col, sinks.astype(F32)[None, :, None, None], table)


ATTN_PATCH = 16


def _attn_kernel(q_ref, kvc_ref, kvp_ref, bias_ref, o_ref, ops_scr):
    n = pl.program_id(1)
    lane = lax.broadcasted_iota(jnp.int32, (1, 2 * HEAD_DIM), 1)
    lo = lane < HEAD_DIM

    def operands(kv):
        kv = kv.astype(F32)
        k_nat = kv[:, :KV_WIDTH]
        v_nat = kv[:, KV_WIDTH:]
        k_swp = pltpu.roll(k_nat, HEAD_DIM, 1)
        v_swp = pltpu.roll(v_nat, HEAD_DIM, 1)
        return (jnp.where(lo, k_nat, 0.0), jnp.where(lo, k_swp, 0.0),
                jnp.where(lo, 0.0, k_swp), jnp.where(lo, 0.0, k_nat),
                jnp.where(lo, v_nat, 1.0), jnp.where(lo, v_swp, 1.0),
                jnp.where(lo, 1.0, v_swp), jnp.where(lo, 1.0, v_nat))

    for idx, op in enumerate(operands(kvp_ref[...])):
        ops_scr[idx, 0:BLOCK, :] = op.astype(BF16)
    for idx, op in enumerate(operands(kvc_ref[...])):
        ops_scr[idx, BLOCK:, :] = op.astype(BF16)

    sink_row = lax.broadcasted_iota(jnp.int32, (ATTN_PATCH, 1), 0) == 0
    zero = jnp.zeros((1, 2 * HEAD_DIM), F32)
    sink_fill = (zero, zero, zero, zero,
                 jnp.where(lo, 0.0, 1.0), jnp.where(lo, 0.0, 1.0),
                 jnp.where(lo, 1.0, 0.0), jnp.where(lo, 1.0, 0.0))

    def window(idx, r0):
        top = ops_scr[idx, pl.ds(r0, ATTN_PATCH), :].astype(F32)
        top = jnp.where(sink_row, sink_fill[idx], top).astype(BF16)
        rest = ops_scr[idx, pl.ds(r0 + ATTN_PATCH, 2 * BLOCK - ATTN_PATCH), :]
        return jnp.concatenate([top, rest], axis=0)

    pairs_per_group = Q_PER_KV // 2
    group_pairs = [range(grp * pairs_per_group, (grp + 1) * pairs_per_group)
                   for grp in range(N_KV_HEADS)]

    def sub_blocks(it, carry):
        r0s = [pl.multiple_of((it * ATTN_GROUP + k) * BLOCK, BLOCK) for k in range(ATTN_GROUP)]
        wins, scores = {}, {}
        for k, r0 in enumerate(r0s):
            first = jnp.where(jnp.logical_and(n == 0, it * ATTN_GROUP + k == 0), 1, 0)
            wins[k] = [window(idx, r0) for idx in range(8)]
            q_all = q_ref[pl.ds(r0, BLOCK), :]
            for grp, pairs in enumerate(group_pairs):
                q_stack = jnp.concatenate(
                    [q_all[:, pr * 2 * HEAD_DIM:(pr + 1) * 2 * HEAD_DIM] for pr in pairs], axis=0)
                for odd in range(2):
                    bias = jnp.concatenate(
                        [bias_ref[first, 2 * pr + odd] for pr in pairs], axis=0)
                    scores[k, grp, odd] = _dot_nt(q_stack, wins[k][2 * odd + grp]) + bias
        probs = {}
        for key, s in scores.items():
            m = jnp.max(s, axis=-1, keepdims=True)
            probs[key] = jnp.exp(s - m).astype(BF16)
        res = {(k, grp, odd): _dot(p, wins[k][4 + 2 * odd + grp])
               for (k, grp, odd), p in probs.items()}
        for k, r0 in enumerate(r0s):
            for grp, pairs in enumerate(group_pairs):
                num = jnp.where(lo, res[k, grp, 0], res[k, grp, 1])
                den = pltpu.roll(jnp.where(lo, res[k, grp, 1], res[k, grp, 0]), HEAD_DIM, 1)
                out = (num / den).astype(BF16)
                for j, pr in enumerate(pairs):
                    o_ref[pl.ds(r0, BLOCK), pr * 2 * HEAD_DIM:(pr + 1) * 2 * HEAD_DIM] = (
                        out[j * BLOCK:(j + 1) * BLOCK, :])
        return carry

    lax.fori_loop(0, ATTN_SUB // ATTN_GROUP, sub_blocks, 0)


def _attention(q, kv, bias):
    bsz, seq, _ = q.shape
    nq = seq // ATTN_TQ
    return pl.pallas_call(
        _attn_kernel,
        grid=(bsz, nq),
        in_specs=[
            pl.BlockSpec((None, ATTN_TQ, ATTN_WIDTH), lambda b, n: (b, n, 0)),
            pl.BlockSpec((None, ATTN_TQ, 2 * KV_WIDTH), lambda b, n: (b, n, 0)),
            pl.BlockSpec((None, BLOCK, 2 * KV_WIDTH),
                         lambda b, n: (b, jnp.maximum(n * ATTN_SUB - 1, 0), 0)),
            _const_spec((2, N_Q_HEADS, BLOCK, 2 * BLOCK)),
        ],
        out_specs=pl.BlockSpec((None, ATTN_TQ, ATTN_WIDTH), lambda b, n: (b, n, 0)),
        out_shape=jax.ShapeDtypeStruct((bsz, seq, ATTN_WIDTH), BF16),
        scratch_shapes=[pltpu.VMEM((8, ATTN_TQ + BLOCK, 2 * HEAD_DIM), BF16)],
        compiler_params=pltpu.CompilerParams(
            dimension_semantics=("arbitrary", "arbitrary"), vmem_limit_bytes=ATTN_VMEM),
        name="attn",
    )(q, kv, kv, bias)


def _scan_permutation():
    p = np.zeros((SCAN_ROWS, SCAN_ROWS), np.float32)
    for c in range(SCAN_CHUNKS):
        for s in range(SCAN_LEN):
            p[s * SCAN_CHUNKS + c, c * SCAN_LEN + s] = 1.0
    return p


def _ssm_kernel(u_ref, perm_ref, permt_ref, wb_ref, wc_ref, a_ref, apow_ref, d_ref,
                wglu_ref, bglu_ref, o_ref, carry_scr, bu_scr, xs_scr, x0_scr, y_scr, *, n_seq):
    @pl.when(pl.program_id(0) == 0)
    def _():
        carry_scr[...] = jnp.zeros_like(carry_scr)

    perm = perm_ref[...]
    u_perm, ub = {}, {}

    def reorder(b):
        u = u_ref[b]
        u_hi = u.astype(BF16)
        u_lo = (u - u_hi.astype(F32)).astype(BF16)
        up_hi = _dot(perm, u_hi)
        u_perm[b] = up_hi + _dot(perm, u_lo)
        ub[b] = up_hi.astype(BF16)

    def project_in(b, lg):
        bu_scr[b, lg] = _dot(ub[b][:, lg * LG_CHANNELS:(lg + 1) * LG_CHANNELS], wb_ref[lg])

    def recurrence(b, lg, a_re, a_im, state, store):
        xr, xi = state
        for s in range(0, SCAN_LEN, 2):
            pair_re, pair_im = [], []
            for t in (s, s + 1):
                rows = slice(t * SCAN_CHUNKS, (t + 1) * SCAN_CHUNKS)
                br = bu_scr[b, lg, rows, :LG_STATES]
                bi = bu_scr[b, lg, rows, LG_STATES:]
                xr, xi = a_re * xr - a_im * xi + br, a_re * xi + a_im * xr + bi
                pair_re.append(xr)
                pair_im.append(xi)
            if store:
                rows = slice(s * SCAN_CHUNKS, (s + 2) * SCAN_CHUNKS)
                xs_scr[b, lg, rows, :LG_STATES] = jnp.concatenate(pair_re, axis=0).astype(BF16)
                xs_scr[b, lg, rows, LG_STATES:] = jnp.concatenate(pair_im, axis=0).astype(BF16)
        return xr, xi

    def scan(b, lg):
        a_re = jnp.broadcast_to(a_ref[lg, 0:1, :], (SCAN_CHUNKS, LG_STATES))
        a_im = jnp.broadcast_to(a_ref[lg, 1:2, :], (SCAN_CHUNKS, LG_STATES))
        zeros = jnp.zeros((SCAN_CHUNKS, LG_STATES), F32)
        e_re, e_im = recurrence(b, lg, a_re, a_im, (zeros, zeros), False)
        p_re = apow_ref[lg, 0:1, :]
        p_im = apow_ref[lg, 1:2, :]
        cur_re = carry_scr[b, lg, 0:1, :]
        cur_im = carry_scr[b, lg, 1:2, :]
        for c in range(SCAN_CHUNKS):
            x0_scr[b, lg, 0, c:c + 1, :] = cur_re
            x0_scr[b, lg, 1, c:c + 1, :] = cur_im
            nxt_re = p_re * cur_re - p_im * cur_im + e_re[c:c + 1, :]
            nxt_im = p_re * cur_im + p_im * cur_re + e_im[c:c + 1, :]
            cur_re, cur_im = nxt_re, nxt_im
        carry_scr[b, lg, 0:1, :] = cur_re
        carry_scr[b, lg, 1:2, :] = cur_im
        recurrence(b, lg, a_re, a_im, (x0_scr[b, lg, 0], x0_scr[b, lg, 1]), True)

    def project_out(b, lg):
        y_scr[b, :, lg * LG_CHANNELS:(lg + 1) * LG_CHANNELS] = _dot(xs_scr[b, lg], wc_ref[lg])

    def finish(b):
        y = y_scr[b] + d_ref[...] * u_perm[b]
        z = _dot(_gelu(y).astype(BF16), wglu_ref[...]) + bglu_ref[...]
        ssm = z[:, :SSM_WIDTH] * jax.nn.sigmoid(z[:, SSM_WIDTH:])
        o_ref[b] = _dot(permt_ref[...], ssm.astype(BF16)).astype(BF16)

    reorder(0)
    project_in(0, 0)
    for lg in range(SSM_LANE_GROUPS):
        for b in range(n_seq):
            scan(b, lg)
            if lg == 0 and b + 1 < n_seq:
                reorder(b + 1)
                project_in(b + 1, 0)
            if lg + 1 < SSM_LANE_GROUPS:
                project_in(b, lg + 1)
            project_out(b, lg)
            if lg + 1 == SSM_LANE_GROUPS:
                finish(b)


def _ssm(u, perm, permt, wb, wc, a, apow, d, wglu, bglu):
    bsz, seq, _ = u.shape
    per_seq = lambda *shape, dt=F32: pltpu.VMEM((bsz,) + shape, dt)
    return pl.pallas_call(
        functools.partial(_ssm_kernel, n_seq=bsz),
        grid=(seq // SCAN_ROWS,),
        in_specs=[
            pl.BlockSpec((bsz, SCAN_ROWS, SSM_WIDTH), lambda s: (0, s, 0)),
            _const_spec((SCAN_ROWS, SCAN_ROWS)),
            _const_spec((SCAN_ROWS, SCAN_ROWS)),
            _const_spec((SSM_LANE_GROUPS, LG_CHANNELS, 2 * LG_STATES)),
            _const_spec((SSM_LANE_GROUPS, 2 * LG_STATES, LG_CHANNELS)),
            _const_spec((SSM_LANE_GROUPS, 2, LG_STATES)),
            _const_spec((SSM_LANE_GROUPS, 2, LG_STATES)),
            _const_spec((1, SSM_WIDTH)),
            _const_spec((SSM_WIDTH, 2 * SSM_WIDTH)),
            _const_spec((1, 2 * SSM_WIDTH)),
        ],
        out_specs=pl.BlockSpec((bsz, SCAN_ROWS, SSM_WIDTH), lambda s: (0, s, 0)),
        out_shape=jax.ShapeDtypeStruct((bsz, seq, SSM_WIDTH), BF16),
        scratch_shapes=[
            per_seq(SSM_LANE_GROUPS, 2, LG_STATES),
            per_seq(SSM_LANE_GROUPS, SCAN_ROWS, 2 * LG_STATES),
            per_seq(SSM_LANE_GROUPS, SCAN_ROWS, 2 * LG_STATES, dt=BF16),
            per_seq(SSM_LANE_GROUPS, 2, SCAN_CHUNKS, LG_STATES),
            per_seq(SCAN_ROWS, SSM_WIDTH),
        ],
        compiler_params=pltpu.CompilerParams(
            dimension_semantics=("arbitrary",), vmem_limit_bytes=SSM_VMEM),
        name="ssm",
    )(u, perm, permt, wb, wc, a, apow, d, wglu, bglu)


def _ssm_params(a_re, a_im, log_dt, b_re, b_im, c_re, c_im):
    dt = jnp.exp(log_dt)[:, None]
    mag = jnp.exp(a_re * dt)
    ab_re = mag * jnp.cos(a_im * dt)
    ab_im = mag * jnp.sin(a_im * dt)
    nr = ab_re - 1.0
    ni = ab_im
    den = a_re * a_re + a_im * a_im
    z_re = (nr * a_re + ni * a_im) / den
    z_im = (ni * a_re - nr * a_im) / den
    bb_re = z_re[..., None] * b_re - z_im[..., None] * b_im
    bb_im = z_re[..., None] * b_im + z_im[..., None] * b_re

    gl = SSM_GROUPS // SSM_LANE_GROUPS
    eye = jnp.eye(gl, dtype=F32)

    def in_mat(bb):
        bb = bb.reshape(SSM_LANE_GROUPS, gl, SSM_STATE, SSM_GROUP)
        m = jnp.einsum('lgph,gk->lghkp', bb, eye)
        return m.reshape(SSM_LANE_GROUPS, LG_CHANNELS, LG_STATES)

    def out_mat(cc):
        cc = cc.reshape(SSM_LANE_GROUPS, gl, SSM_GROUP, SSM_STATE)
        m = jnp.einsum('lghp,gk->lgpkh', cc, eye)
        return m.reshape(SSM_LANE_GROUPS, LG_STATES, LG_CHANNELS)

    wb = jnp.concatenate([in_mat(bb_re), in_mat(bb_im)], axis=-1).astype(BF16)
    wc = jnp.concatenate([out_mat(c_re), out_mat(-c_im)], axis=-2).astype(BF16)

    pr, pi = ab_re, ab_im
    for _ in range(int(math.log2(SCAN_LEN))):
        pr, pi = pr * pr - pi * pi, 2.0 * pr * pi
    lanes = lambda v: v.reshape(SSM_LANE_GROUPS, LG_STATES)
    a = jnp.stack([lanes(ab_re), lanes(ab_im)], axis=1)
    apow = jnp.stack([lanes(pr), lanes(pi)], axis=1)
    return wb, wc, a, apow


def _merge_kernel(attn_ref, ssm_ref, ga_ref, gs_ref, x_ref, wba_ref, wbs_ref, wout_ref,
                  g_ref, x1_ref, h2_ref):
    a = _dot(attn_ref[...], wba_ref[...])
    s = _dot(ssm_ref[...], wbs_ref[...])
    merged = jax.nn.sigmoid(ga_ref[...]) * a + jax.nn.sigmoid(gs_ref[...]) * s
    x1 = x_ref[...] + _dot(merged.astype(BF16), wout_ref[...])
    x1_ref[...] = x1
    h2_ref[...] = _rms_norm(x1, g_ref[...]).astype(BF16)


def _merge(attn, ssm, ga, gs, x2, wba, wbs, wout, g, tm):
    t = x2.shape[0]
    row = lambda width: pl.BlockSpec((tm, width), lambda i: (i, 0))
    return pl.pallas_call(
        _merge_kernel,
        grid=(t // tm,),
        in_specs=[row(ATTN_WIDTH), row(SSM_WIDTH), row(D_MODEL), row(D_MODEL), row(D_MODEL),
                  _const_spec((ATTN_WIDTH, D_MODEL)), _const_spec((SSM_WIDTH, D_MODEL)),
                  _const_spec((D_MODEL, D_MODEL)), _const_spec((1, D_MODEL))],
        out_specs=[row(D_MODEL), row(D_MODEL)],
        out_shape=[jax.ShapeDtypeStruct((t, D_MODEL), F32),
                   jax.ShapeDtypeStruct((t, D_MODEL), BF16)],
        compiler_params=pltpu.CompilerParams(
            dimension_semantics=("arbitrary",), vmem_limit_bytes=MERGE_VMEM),
        name="merge",
    )(attn, ssm, ga, gs, x2, wba, wbs, wout, g)


FFN_TAIL = V7X_SUBLANES
FFN_CHUNK = 256
FFN_ROW_GROUPS = 2


def _ffn_kernel(h_ref, x1_hbm, wv_ref, wg_ref, cw_ref, cb_ref, wd_ref, gf_ref,
                o_ref, x1_scr, act_scr, tail_scr, x1_sem, *, tm, tf, seq):
    i = pl.program_id(0)
    j = pl.program_id(1)
    last = pl.num_programs(1) - 1

    def x1_copy():
        return pltpu.make_async_copy(x1_hbm.at[pl.ds(i * tm, tm), :], x1_scr, x1_sem)

    @pl.when(jnp.logical_and(i == 0, j == 0))
    def _():
        tail_scr[...] = jnp.zeros_like(tail_scr)

    @pl.when(j == 0)
    def _():
        x1_copy().start()
        o_ref[...] = jnp.zeros_like(o_ref)

    seq_start = (i * tm) % seq == 0
    row = lax.broadcasted_iota(jnp.int32, (FFN_TAIL, FFN_CHUNK), 0)
    n_chunks = tf // FFN_CHUNK
    rh = tm // FFN_ROW_GROUPS
    col = lambda c: slice(c * FFN_CHUNK, (c + 1) * FFN_CHUNK)
    ups = {}
    for r in range(FFN_ROW_GROUPS):
        h = h_ref[r * rh:(r + 1) * rh, :]
        for c in range(n_chunks):
            ups[r, c] = (_dot(h, wv_ref[:, col(c)]), _dot(h, wg_ref[:, col(c)]))
    for c in range(n_chunks):
        prev = jnp.where(seq_start, 0.0, tail_scr[j, :, col(c)])
        for r in range(FFN_ROW_GROUPS):
            val, gate = ups[r, c]
            conv = cb_ref[:, col(c)] + cw_ref[CONV_WIDTH - 1:CONV_WIDTH, col(c)] * gate
            for back in range(1, CONV_WIDTH):
                shifted = pltpu.roll(gate, back, 0)
                top = jnp.where(row < back, pltpu.roll(prev, back, 0), shifted[:FFN_TAIL])
                shifted = jnp.concatenate([top, shifted[FFN_TAIL:]], axis=0)
                k = CONV_WIDTH - 1 - back
                conv = conv + cw_ref[k:k + 1, col(c)] * shifted
            prev = gate[rh - FFN_TAIL:, :]
            act_scr[r * rh:(r + 1) * rh, col(c)] = (val * _gelu(conv)).astype(BF16)
        tail_scr[j, :, col(c)] = prev
    for r in range(FFN_ROW_GROUPS):
        rows = slice(r * rh, (r + 1) * rh)
        o_ref[rows, :] += _dot(act_scr[rows, :], wd_ref[...])

    @pl.when(j == last)
    def _():
        x1_copy().wait()
        o_ref[...] = _rms_norm(x1_scr[...] + o_ref[...], gf_ref[...])


def _ffn(h2, x1, w_up, conv_w, conv_b, w_down, gf, seq, tm, tf):
    t = h2.shape[0]
    nf = D_FF // tf
    kern = functools.partial(_ffn_kernel, tm=tm, tf=tf, seq=seq)
    return pl.pallas_call(
        kern,
        grid=(t // tm, nf),
        in_specs=[
            pl.BlockSpec((tm, D_MODEL), lambda i, j: (i, 0)),
            pl.BlockSpec(memory_space=pl.ANY),
            pl.BlockSpec((D_MODEL, tf), lambda i, j: (0, j)),
            pl.BlockSpec((D_MODEL, tf), lambda i, j: (0, j + nf)),
            pl.BlockSpec((CONV_WIDTH, tf), lambda i, j: (0, j)),
            pl.BlockSpec((1, tf), lambda i, j: (0, j)),
            pl.BlockSpec((tf, D_MODEL), lambda i, j: (j, 0)),
            _const_spec((1, D_MODEL)),
        ],
        out_specs=pl.BlockSpec((tm, D_MODEL), lambda i, j: (i, 0)),
        out_shape=jax.ShapeDtypeStruct((t, D_MODEL), F32),
        scratch_shapes=[pltpu.VMEM((tm, D_MODEL), F32),
                        pltpu.VMEM((tm, tf), BF16),
                        pltpu.VMEM((nf, FFN_TAIL, tf), F32),
                        pltpu.SemaphoreType.DMA(())],
        compiler_params=pltpu.CompilerParams(
            dimension_semantics=("arbitrary", "arbitrary"), vmem_limit_bytes=FFN_VMEM),
        name="ffn",
    )(h2, x1, w_up, w_up, conv_w, conv_b, w_down, gf)


def _layer(layer, x, attn_norm_g, w_in, b_in, attn_sinks, ssm_a_re, ssm_a_im, ssm_log_dt,
           ssm_b_re, ssm_b_im, ssm_c_re, ssm_c_im, ssm_d, b_glu, ffn_norm_g, conv_w, conv_b,
           out_norm_g, stacked_weights):
    bsz, seq, width = x.shape
    t = bsz * seq
    assert width == D_MODEL and x.dtype == F32
    assert seq % ATTN_TQ == 0 and seq % SCAN_ROWS == 0 and seq % FFN_ROWS == 0
    assert t % IN_PROJ_ROWS == 0 and t % MERGE_ROWS == 0
    assert D_FF % FFN_DFF_TILE == 0 and FFN_DFF_TILE % FFN_CHUNK == 0
    x2 = x.reshape(t, D_MODEL)
    row = lambda v: v.reshape(1, -1)

    (q, kv, u, ga, gs), (w_glu, w_branch_attn, w_branch_ssm, w_out, w_up, w_down) = _in_proj(
        x2, row(attn_norm_g), w_in, row(b_in), stacked_weights, layer, tm=IN_PROJ_ROWS)

    attn = _attention(q.reshape(bsz, seq, ATTN_WIDTH), kv.reshape(bsz, seq, 2 * KV_WIDTH),
                      _attn_bias_table(attn_sinks))

    wb, wc, a, apow = _ssm_params(ssm_a_re, ssm_a_im, ssm_log_dt, ssm_b_re, ssm_b_im,
                                  ssm_c_re, ssm_c_im)
    perm = _scan_permutation()
    ssm = _ssm(u.reshape(bsz, seq, SSM_WIDTH), jnp.asarray(perm, BF16), jnp.asarray(perm.T, BF16),
               wb, wc, a, apow, row(ssm_d), w_glu, row(b_glu))

    x1, h2 = _merge(attn.reshape(t, ATTN_WIDTH), ssm.reshape(t, SSM_WIDTH), ga, gs, x2,
                    w_branch_attn, w_branch_ssm, w_out, row(ffn_norm_g), tm=MERGE_ROWS)

    out = _ffn(h2, x1, w_up, conv_w, row(conv_b), w_down, row(out_norm_g),
               seq=seq, tm=FFN_ROWS, tf=FFN_DFF_TILE)
    return out.reshape(bsz, seq, D_MODEL)


def kernel(x, attn_norm_g, w_in, b_in, attn_sinks, ssm_a_re, ssm_a_im, ssm_log_dt, ssm_b_re,
           ssm_b_im, ssm_c_re, ssm_c_im, ssm_d, w_glu, b_glu, w_branch_attn, w_branch_ssm, w_out,
           ffn_norm_g, w_up, conv_w, conv_b, w_down, final_norm_g):
    depth = w_in.shape[0]
    assert depth == 1, "the fused final norm assumes a single layer"
    return _layer(0, x, attn_norm_g[0], w_in, b_in[0], attn_sinks[0], ssm_a_re[0], ssm_a_im[0],
                  ssm_log_dt[0], ssm_b_re[0], ssm_b_im[0], ssm_c_re[0], ssm_c_im[0], ssm_d[0],
                  b_glu[0], ffn_norm_g[0], conv_w[0], conv_b[0], final_norm_g,
                  [w_glu, w_branch_attn, w_branch_ssm, w_out, w_up, w_down])
```
